```python
import math
import jax, jax.numpy as jnp
from jax import lax
import numpy as np

D_MODEL = 1024
BATCH = 8
SEQ = 8192
DEPTH = 1

CONV_WIDTH = D_MODEL // 2
CONV_K = 3
N_HEADS = 8
HEAD_DIM = 64
ATTN_WIDTH = N_HEADS * HEAD_DIM
MOBA_BLOCK = 256
MOBA_TOPK = 3
QUERY_CHUNK = 32
N_BUCKETS = 32
MAX_DISTANCE = 1024
D_FF = 4 * D_MODEL
RMS_EPS = 1e-6
N_IN = 3 * CONV_WIDTH + 3 * ATTN_WIDTH + 2 * D_MODEL
NEG = -1e30

kernel_name = "hybrid_shortconv_moba_sqrelu_block"


def rms_norm(x, g):
    x32 = x.astype(jnp.float32)
    y = x32 * lax.rsqrt(jnp.mean(jnp.square(x32), axis=-1, keepdims=True) + RMS_EPS)
    return (y * g.astype(jnp.float32)).astype(x.dtype)


def t5_causal_bucket(rel):
    n = jnp.maximum(rel, 0)
    max_exact = N_BUCKETS // 2
    nf = jnp.maximum(n, max_exact).astype(jnp.float32)
    large = max_exact + (jnp.log(nf / max_exact) / math.log(MAX_DISTANCE / max_exact)
                         * (N_BUCKETS - max_exact)).astype(jnp.int32)
    large = jnp.minimum(large, N_BUCKETS - 1)
    return jnp.where(n < max_exact, n, large)


def short_conv(u, w):
    return lax.conv_general_dilated(
        u, w[:, None, :], window_strides=(1,), padding=[(CONV_K - 1, 0)],
        dimension_numbers=("NWC", "WIO", "NWC"), feature_group_count=u.shape[-1])


def moba_attention(q, k, v, rel_bias):
    b, s, h, d = q.shape
    n_blocks = max(-(-s // MOBA_BLOCK), MOBA_TOPK)
    s_pad = n_blocks * MOBA_BLOCK
    q = q.transpose(0, 2, 1, 3)
    pad = ((0, 0), (0, 0), (0, s_pad - s), (0, 0))
    k = jnp.pad(k.transpose(0, 2, 1, 3), pad)
    v = jnp.pad(v.transpose(0, 2, 1, 3), pad)
    k_blocks = k.reshape(b, h, n_blocks, MOBA_BLOCK, d)
    v_blocks = v.reshape(b, h, n_blocks, MOBA_BLOCK, d)
    k_mean = jnp.mean(k_blocks, axis=3)
    scale = HEAD_DIM ** -0.5
    bi = jnp.arange(b)[:, None, None, None]
    hi = jnp.arange(h)[None, :, None, None]
    blk_ids = jnp.arange(n_blocks)
    offs = jnp.arange(MOBA_BLOCK)
    n_sel = MOBA_TOPK * MOBA_BLOCK

    def chunk(c):
        start = c * QUERY_CHUNK
        blk = start // MOBA_BLOCK
        qc = lax.dynamic_slice_in_dim(q, start, QUERY_CHUNK, axis=2)
        qpos = start + jnp.arange(QUERY_CHUNK)
        gate = jnp.einsum('bhqd,bhnd->bhqn', qc, k_mean).astype(jnp.float32)
        gate = jnp.where(blk_ids < blk, gate, NEG)
        _, sel = lax.top_k(gate, MOBA_TOPK)
        sel_valid = sel < blk
        kg = k_blocks[bi, hi, sel]
        vg = v_blocks[bi, hi, sel]
        kpos = sel[..., None] * MOBA_BLOCK + offs
        rel = qpos[None, None, :, None, None] - kpos
        bias_past = rel_bias[hi[..., None], t5_causal_bucket(rel)]
        s_past = jnp.einsum('bhqd,bhqjkd->bhqjk', qc, kg).astype(jnp.float32) * scale + bias_past
        s_past = jnp.where(sel_valid[..., None], s_past, NEG).reshape(b, h, QUERY_CHUNK, n_sel)
        kb = lax.dynamic_slice_in_dim(k, blk * MOBA_BLOCK, MOBA_BLOCK, axis=2)
        vb = lax.dynamic_slice_in_dim(v, blk * MOBA_BLOCK, MOBA_BLOCK, axis=2)
        rel_own = qpos[:, None] - (blk * MOBA_BLOCK + offs)[None, :]
        bias_own = rel_bias[:, t5_causal_bucket(rel_own)]
        s_own = jnp.einsum('bhqd,bhkd->bhqk', qc, kb).astype(jnp.float32) * scale + bias_own
        s_own = jnp.where(rel_own >= 0, s_own, NEG)
        p = jax.nn.softmax(jnp.concatenate([s_past, s_own], axis=-1), axis=-1).astype(v.dtype)
        p_past = p[..., :n_sel].reshape(b, h, QUERY_CHUNK, MOBA_TOPK, MOBA_BLOCK)
        p_own = p[..., n_sel:]
        return (jnp.einsum('bhqjk,bhqjkd->bhqd', p_past, vg)
                + jnp.einsum('bhqk,bhkd->bhqd', p_own, vb))

    outs = lax.map(chunk, jnp.arange(s // QUERY_CHUNK))
    return outs.transpose(1, 0, 3, 2, 4).reshape(b, s, h * d)


def setup_inputs(seed: int = 0) -> dict:
    key = jax.random.key(seed)
    ks = jax.random.split(key, 14)
    nrm = jax.random.normal
    f32 = jnp.float32

    def gain(k):
        return 1.0 + 0.02 * nrm(k, (DEPTH, D_MODEL), f32)

    return {
        "x": nrm(ks[0], (BATCH, SEQ, D_MODEL), f32),
        "ln_mix_pre": gain(ks[1]),
        "w_in": nrm(ks[2], (DEPTH, D_MODEL, N_IN), f32) * D_MODEL ** -0.5,
        "conv_w": nrm(ks[3], (DEPTH, CONV_K, CONV_WIDTH), f32) * CONV_K ** -0.5,
        "w_conv_out": nrm(ks[4], (DEPTH, CONV_WIDTH, D_MODEL), f32) * CONV_WIDTH ** -0.5,
        "w_attn_out": nrm(ks[5], (DEPTH, ATTN_WIDTH, D_MODEL), f32) * ATTN_WIDTH ** -0.5,
        "rel_bias": 0.2 * nrm(ks[6], (N_HEADS, N_BUCKETS), f32),
        "w_o": nrm(ks[7], (DEPTH, D_MODEL, D_MODEL), f32) * D_MODEL ** -0.5,
        "ln_mix_post": gain(ks[8]),
        "ln_mlp_pre": gain(ks[9]),
        "w_mlp_in": nrm(ks[10], (DEPTH, D_MODEL, D_FF), f32) * D_MODEL ** -0.5,
        "w_mlp_out": nrm(ks[11], (DEPTH, D_FF, D_MODEL), f32) * D_FF ** -0.5,
        "ln_mlp_post": gain(ks[12]),
    }


def reference(x, ln_mix_pre, w_in, conv_w, w_conv_out, w_attn_out, rel_bias, w_o,
              ln_mix_post, ln_mlp_pre, w_mlp_in, w_mlp_out, ln_mlp_post):
    b, s, _ = x.shape
    c, a, dm = CONV_WIDTH, ATTN_WIDTH, D_MODEL
    splits = [c, 2 * c, 3 * c, 3 * c + a, 3 * c + 2 * a, 3 * c + 3 * a, 3 * c + 3 * a + dm]
    for l in range(DEPTH):
        h = rms_norm(x, ln_mix_pre[l])
        proj = h @ w_in[l]
        xin, gb, gc, q, k, v, g_conv, g_attn = jnp.split(proj, splits, axis=-1)
        y_c = (gb * short_conv(gc * xin, conv_w[l])) @ w_conv_out[l]
        att = moba_attention(q.reshape(b, s, N_HEADS, HEAD_DIM),
                             k.reshape(b, s, N_HEADS, HEAD_DIM),
                             v.reshape(b, s, N_HEADS, HEAD_DIM), rel_bias)
        y_a = att @ w_attn_out[l]
        m = jax.nn.sigmoid(g_conv) * y_c + jax.nn.sigmoid(g_attn) * y_a
        x = x + rms_norm(m @ w_o[l], ln_mix_post[l])
        h = rms_norm(x, ln_mlp_pre[l])
        f = jnp.square(jax.nn.relu(h @ w_mlp_in[l])) @ w_mlp_out[l]
        x = x + rms_norm(f, ln_mlp_post[l])
    return x
```

```python
import functools
import math

import jax
import jax.numpy as jnp
from jax import lax
from jax.experimental import pallas as pl
from jax.experimental.pallas import tpu as pltpu

D_MODEL = 1024
CONV_WIDTH = D_MODEL // 2
CONV_K = 3
N_HEADS = 8
HEAD_DIM = 64
ATTN_WIDTH = N_HEADS * HEAD_DIM
MOBA_BLOCK = 256
MOBA_TOPK = 3
N_BUCKETS = 32
MAX_DISTANCE = 1024
D_FF = 4 * D_MODEL
RMS_EPS = 1e-6
N_IN = 3 * CONV_WIDTH + 3 * ATTN_WIDTH + 2 * D_MODEL
NEG = -1e30

LANES = 128
AUG = 2 * HEAD_DIM
N_BIAS_TILES = 6
TOKEN_TILE = 512
FF_CHUNK = 1024
VMEM_LIMIT = 56 * 1024 * 1024

F32 = jnp.float32
BF16 = jnp.bfloat16


def _rms(x, g):
    return (x * lax.rsqrt(jnp.mean(x * x, axis=-1, keepdims=True) + RMS_EPS)) * g


def _t5_causal_bucket(rel):
    n = jnp.maximum(rel, 0)
    max_exact = N_BUCKETS // 2
    nf = jnp.maximum(n, max_exact).astype(F32)
    large = max_exact + (jnp.log(nf / max_exact) / math.log(MAX_DISTANCE / max_exact)
                         * (N_BUCKETS - max_exact)).astype(jnp.int32)
    large = jnp.minimum(large, N_BUCKETS - 1)
    return jnp.where(n < max_exact, n, large)


def _bias_tiles(rel_bias):
    d = jnp.arange(N_BIAS_TILES)[:, None, None]
    qi = jnp.arange(MOBA_BLOCK)[None, :, None]
    kj = jnp.arange(MOBA_BLOCK)[None, None, :]
    rel = d * MOBA_BLOCK + qi - kj
    bias = rel_bias[:, _t5_causal_bucket(rel)]
    return jnp.where(rel >= 0, bias, NEG).astype(F32)


def _proj_kernel(x_ref, g_ref, win_ref, cw_ref, wco_ref,
                 qt_ref, k_ref, v_ref, gc_ref, sga_ref,
                 carry_ref, kmean_ref, *, n_blocks):
    tm = x_ref.shape[1]
    c, a, d = CONV_WIDTH, ATTN_WIDTH, D_MODEL
    s = pl.program_id(1)
    blocks_per_tile = tm // MOBA_BLOCK
    blk0 = s * blocks_per_tile

    @pl.when(s == 0)
    def _():
        carry_ref[...] = jnp.zeros_like(carry_ref)
        kmean_ref[...] = jnp.zeros_like(kmean_ref)

    h = _rms(x_ref[0], g_ref[...]).astype(BF16)

    def seg(lo, width):
        return jnp.dot(h, win_ref[:, lo:lo + width], preferred_element_type=F32)

    u = seg(2 * c, c) * seg(0, c)
    row = lax.broadcasted_iota(jnp.int32, u.shape, 0)
    prev = carry_ref[...]
    p1, p2 = prev[7:8], prev[6:7]
    u1 = jnp.where(row == 0, p1, pltpu.roll(u, 1, 0))
    u2 = jnp.where(row == 0, p2, jnp.where(row == 1, p1, pltpu.roll(u, 2, 0)))
    carry_ref[...] = u[tm - 8:]
    cw = cw_ref[...]
    conv = cw[2:3] * u + cw[1:2] * u1 + cw[0:1] * u2
    z = (seg(c, c) * conv).astype(BF16)
    yc = jnp.dot(z, wco_ref[...], preferred_element_type=F32)
    gc_ref[0] = jax.nn.sigmoid(seg(3 * c + 3 * a, d)) * yc
    sga_ref[0] = jax.nn.sigmoid(seg(3 * c + 3 * a + d, d))

    q = seg(3 * c, a)
    k = seg(3 * c + a, a)
    v = seg(3 * c + 2 * a, a)

    rows_nb = lax.broadcasted_iota(jnp.int32, (n_blocks, a), 0)
    km = kmean_ref[...]
    for bi in range(blocks_per_tile):
        ks = jnp.sum(k[bi * MOBA_BLOCK:(bi + 1) * MOBA_BLOCK], axis=0, keepdims=True) * (1.0 / MOBA_BLOCK)
        km = jnp.where(rows_nb == blk0 + bi, ks, km)
    kmean_ref[...] = km

    qt = q.T
    n_iota = lax.broadcasted_iota(jnp.int32, (n_blocks, tm), 0)
    tok_blk = blk0 + lax.broadcasted_iota(jnp.int32, (n_blocks, tm), 1) // MOBA_BLOCK
    past = n_iota < tok_blk
    zeros_pad = jnp.zeros((AUG - HEAD_DIM - n_blocks, tm), F32)
    for hd in range(N_HEADS):
        qth = qt[hd * HEAD_DIM:(hd + 1) * HEAD_DIM]
        gate = jnp.dot(km[:, hd * HEAD_DIM:(hd + 1) * HEAD_DIM], qth,
                       precision=lax.Precision.HIGHEST, preferred_element_type=F32)
        g = jnp.where(past, gate, NEG)
        sel = n_iota == tok_blk
        for _ in range(MOBA_TOPK):
            m = jnp.max(g, axis=0, keepdims=True)
            idx = jnp.min(jnp.where(g == m, n_iota, n_blocks), axis=0, keepdims=True)
            pick = n_iota == idx
            sel = sel | (pick & past)
            g = jnp.where(pick, -jnp.inf, g)
        mask_t = jnp.where(sel, 0.0, NEG)
        qt_ref[0, hd] = jnp.concatenate(
            [qth * (HEAD_DIM ** -0.5), mask_t, zeros_pad], axis=0).astype(BF16)

    lane = lax.broadcasted_iota(jnp.int32, (tm, LANES), 1)
    row_blk = blk0 + lax.broadcasted_iota(jnp.int32, (tm, LANES), 0) // MOBA_BLOCK
    k_pat = jnp.where(lane - HEAD_DIM == row_blk, 1.0, 0.0)
    v_pat = jnp.where(lane == HEAD_DIM, 1.0, 0.0)
    low = lane < HEAD_DIM
    for hp in range(N_HEADS // 2):
        kt = k[:, hp * LANES:(hp + 1) * LANES]
        vt = v[:, hp * LANES:(hp + 1) * LANES]
        k_ref[0, 2 * hp] = jnp.where(low, kt, k_pat).astype(BF16)
        k_ref[0, 2 * hp + 1] = jnp.where(low, pltpu.roll(kt, HEAD_DIM, 1), k_pat).astype(BF16)
        v_ref[0, 2 * hp] = jnp.where(low, vt, v_pat).astype(BF16)
        v_ref[0, 2 * hp + 1] = jnp.where(low, pltpu.roll(vt, HEAD_DIM, 1), v_pat).astype(BF16)


def _projection(x, g, w_in, conv_w, w_conv_out):
    b, s, d = x.shape
    tm = TOKEN_TILE
    n_blocks = s // MOBA_BLOCK
    const = lambda shape: pl.BlockSpec(shape, lambda bi, si: (0,) * len(shape),
                                       pipeline_mode=pl.Buffered(1))
    head_tok = pl.BlockSpec((1, N_HEADS, tm, AUG), lambda bi, si: (bi, 0, si, 0))
    tok = pl.BlockSpec((1, tm, d), lambda bi, si: (bi, si, 0))
    return pl.pallas_call(
        functools.partial(_proj_kernel, n_blocks=n_blocks),
        grid=(b, s // tm),
        in_specs=[tok, const((1, d)), const((d, N_IN)), const((CONV_K, CONV_WIDTH)),
                  const((CONV_WIDTH, d))],
        out_specs=[pl.BlockSpec((1, N_HEADS, AUG, tm), lambda bi, si: (bi, 0, 0, si)),
                   head_tok, head_tok, tok, tok],
        out_shape=[jax.ShapeDtypeStruct((b, N_HEADS, AUG, s), BF16),
                   jax.ShapeDtypeStruct((b, N_HEADS, s, AUG), BF16),
                   jax.ShapeDtypeStruct((b, N_HEADS, s, AUG), BF16),
                   jax.ShapeDtypeStruct((b, s, d), F32),
                   jax.ShapeDtypeStruct((b, s, d), F32)],
        scratch_shapes=[pltpu.VMEM((8, CONV_WIDTH), F32),
                        pltpu.VMEM((n_blocks, ATTN_WIDTH), F32)],
        compiler_params=pltpu.CompilerParams(
            dimension_semantics=("arbitrary", "arbitrary"), vmem_limit_bytes=VMEM_LIMIT),
        name="moba_projection",
    )(x, g, w_in, conv_w, w_conv_out)


def _attn_kernel(qt_ref, k_ref, v_ref, bias_ref, o_ref):
    i = pl.program_id(2)
    outs = []
    for hh in range(2):
        q = qt_ref[0, hh].astype(F32).T.astype(BF16)

        def body(j, carry, hh=hh, q=q):
            m, acc = carry
            rows = pl.ds(pl.multiple_of(j * MOBA_BLOCK, MOBA_BLOCK), MOBA_BLOCK)
            kj = k_ref[0, hh, rows, :]
            vj = v_ref[0, hh, rows, :]
            sc = lax.dot_general(q, kj, (((1,), (1,)), ((), ())), preferred_element_type=F32)
            sc = sc + bias_ref[hh, jnp.minimum(i - j, N_BIAS_TILES - 1)]
            m_new = jnp.maximum(m, jnp.max(sc, axis=-1, keepdims=True))
            p = jnp.exp(sc - m_new)
            acc = jnp.exp(m - m_new) * acc + jnp.dot(p.astype(BF16), vj, preferred_element_type=F32)
            return m_new, acc

        m0 = jnp.full((MOBA_BLOCK, 1), -jnp.inf, F32)
        acc0 = jnp.zeros((MOBA_BLOCK, AUG), F32)
        _, acc = lax.fori_loop(0, i + 1, body, (m0, acc0))
        outs.append(acc[:, :HEAD_DIM] / acc[:, HEAD_DIM:HEAD_DIM + 1])
    o_ref[0] = jnp.concatenate(outs, axis=-1).astype(o_ref.dtype)


def _attention(qt, k_aug, v_aug, bias):
    b, _, _, s = qt.shape
    n_blocks = s // MOBA_BLOCK
    kv_spec = pl.BlockSpec((1, 2, s, AUG), lambda bi, hp, i: (bi, hp, 0, 0))
    return pl.pallas_call(
        _attn_kernel,
        grid=(b, N_HEADS // 2, n_blocks),
        in_specs=[pl.BlockSpec((1, 2, AUG, MOBA_BLOCK), lambda bi, hp, i: (bi, hp, 0, i)),
                  kv_spec, kv_spec,
                  pl.BlockSpec((2, N_BIAS_TILES, MOBA_BLOCK, MOBA_BLOCK),
                               lambda bi, hp, i: (hp, 0, 0, 0))],
        out_specs=pl.BlockSpec((1, MOBA_BLOCK, 2 * HEAD_DIM), lambda bi, hp, i: (bi, i, hp)),
        out_shape=jax.ShapeDtypeStruct((b, s, ATTN_WIDTH), BF16),
        compiler_params=pltpu.CompilerParams(
            dimension_semantics=("arbitrary", "arbitrary", "arbitrary"),
            vmem_limit_bytes=VMEM_LIMIT),
        name="moba_attention",
    )(qt, k_aug, v_aug, bias)


def _out_kernel(x_ref, att_ref, gc_ref, sga_ref, wao_ref, wo_ref, gpost_ref, gpre_ref,
                w1_ref, w2_ref, gmpost_ref, o_ref):
    ya = jnp.dot(att_ref[0], wao_ref[...], preferred_element_type=F32)
    m = (gc_ref[0] + sga_ref[0] * ya).astype(BF16)
    mix = jnp.dot(m, wo_ref[...], preferred_element_type=F32)
    x1 = x_ref[0] + _rms(mix, gpost_ref[...])
    h2 = _rms(x1, gpre_ref[...]).astype(BF16)
    f = jnp.zeros(x1.shape, F32)
    for ci in range(D_FF // FF_CHUNK):
        cols = slice(ci * FF_CHUNK, (ci + 1) * FF_CHUNK)
        act = jnp.dot(h2, w1_ref[:, cols], preferred_element_type=F32)
        act = jnp.square(jnp.maximum(act, 0.0)).astype(BF16)
        f = f + jnp.dot(act, w2_ref[cols, :], preferred_element_type=F32)
    o_ref[0] = x1 + _rms(f, gmpost_ref[...])


def _output(x, att, gc, sga, w_attn_out, w_o, g_post, g_mlp_pre, w1, w2, g_mlp_post):
    b, s, d = x.shape
    tm = TOKEN_TILE
    const = lambda shape: pl.BlockSpec(shape, lambda bi, si: (0,) * len(shape),
                                       pipeline_mode=pl.Buffered(1))
    tok = pl.BlockSpec((1, tm, d), lambda bi, si: (bi, si, 0))
    return pl.pallas_call(
        _out_kernel,
        grid=(b, s // tm),
        in_specs=[tok, pl.BlockSpec((1, tm, ATTN_WIDTH), lambda bi, si: (bi, si, 0)), tok, tok,
                  const((ATTN_WIDTH, d)), const((d, d)), const((1, d)), const((1, d)),
                  const((d, D_FF)), const((D_FF, d)), const((1, d))],
        out_specs=tok,
        out_shape=jax.ShapeDtypeStruct((b, s, d), F32),
        compiler_params=pltpu.CompilerParams(
            dimension_semantics=("arbitrary", "arbitrary"), vmem_limit_bytes=VMEM_LIMIT),
        name="moba_output_mlp",
    )(x, att, gc, sga, w_attn_out, w_o, g_post, g_mlp_pre, w1, w2, g_mlp_post)


def kernel(x, ln_mix_pre, w_in, conv_w, w_conv_out, w_attn_out, rel_bias, w_o, ln_mix_post,
           ln_mlp_pre, w_mlp_in, w_mlp_out, ln_mlp_post):
    depth = w_in.shape[0]
    b, s, d = x.shape
    assert d == D_MODEL and s % TOKEN_TILE == 0 and TOKEN_TILE % MOBA_BLOCK == 0
    assert s // MOBA_BLOCK >= MOBA_TOPK and s // MOBA_BLOCK <= AUG - HEAD_DIM
    bias = _bias_tiles(rel_bias)
    for l in range(depth):
        qt, k_aug, v_aug, gc, sga = _projection(
            x, ln_mix_pre[l][None], w_in[l].astype(BF16), conv_w[l], w_conv_out[l].astype(BF16))
        att = _attention(qt, k_aug, v_aug, bias)
        x = _output(x, att, gc, sga, w_attn_out[l].astype(BF16), w_o[l].astype(BF16),
                    ln_mix_post[l][None], ln_mlp_pre[l][None], w_mlp_in[l].astype(BF16),
                    w_mlp_out[l].astype(BF16), ln_mlp_post[l][None])
    return x
```

```python
import functools
import math

import jax
import jax.numpy as jnp
from jax import lax
from jax.experimental import pallas as pl
from jax.experimental.pallas import tpu as pltpu

D_MODEL = 1024
CONV_WIDTH = D_MODEL // 2
CONV_K = 3
N_HEADS = 8
HEAD_DIM = 64
ATTN_WIDTH = N_HEADS * HEAD_DIM
MOBA_BLOCK = 256
MOBA_TOPK = 3
N_BUCKETS = 32
MAX_DISTANCE = 1024
D_FF = 4 * D_MODEL
RMS_EPS = 1e-6
N_IN = 3 * CONV_WIDTH + 3 * ATTN_WIDTH + 2 * D_MODEL
NEG = -1e30
LOG2E = math.log2(math.e)

LANES = 128
BF16_SUBLANES = 16
AUG = 2 * HEAD_DIM
MASK_SLOTS = (AUG - HEAD_DIM) // 2
V_ROWS = HEAD_DIM + BF16_SUBLANES
SUPER = 2 * MOBA_BLOCK
FAR_DIST = -(-(MAX_DISTANCE + MOBA_BLOCK) // MOBA_BLOCK)
N_NEAR = (FAR_DIST + 1) // 2
TOKEN_TILE = 512
FF_CHUNK = 1024
VMEM_LIMIT = 56 * 1024 * 1024

F32 = jnp.float32
BF16 = jnp.bfloat16


def _rms(x, g):
    return (x * lax.rsqrt(jnp.mean(x * x, axis=-1, keepdims=True) + RMS_EPS)) * g


def _t5_causal_bucket(rel):
    n = jnp.maximum(rel, 0)
    max_exact = N_BUCKETS // 2
    nf = jnp.maximum(n, max_exact).astype(F32)
    large = max_exact + (jnp.log(nf / max_exact) / math.log(MAX_DISTANCE / max_exact)
                         * (N_BUCKETS - max_exact)).astype(jnp.int32)
    large = jnp.minimum(large, N_BUCKETS - 1)
    return jnp.where(n < max_exact, n, large)


def _bias_tiles(rel_bias):
    h = rel_bias.shape[0]
    n = N_NEAR + 1
    m = jnp.arange(2 * SUPER)
    delta = jnp.arange(n)[:, None]
    rel = delta * SUPER + jnp.where(m < SUPER, m, m - 2 * SUPER)
    w = jnp.where(rel >= 0, rel_bias[:, _t5_causal_bucket(rel)] * LOG2E, NEG).astype(F32)
    t = jnp.broadcast_to(w[:, :, None, :], (h, n, SUPER, 2 * SUPER)).reshape(h, n, -1)
    t = t[:, :, :SUPER * (2 * SUPER - 1)].reshape(h, n, SUPER, 2 * SUPER - 1)[..., :SUPER]
    kb = jnp.arange(SUPER)[:, None] // MOBA_BLOCK
    qb = jnp.arange(SUPER)[None, :] // MOBA_BLOCK
    dist = (SUPER // MOBA_BLOCK) * delta[:, :, None] + qb - kb
    return jnp.where(dist >= FAR_DIST, 0.0, t)


def _far_bias(rel_bias):
    c = rel_bias[:, _t5_causal_bucket(jnp.int32(FAR_DIST * MOBA_BLOCK))] * LOG2E
    hi = c.astype(BF16).astype(F32)
    return hi, c - hi


def _proj_kernel(far_hi_ref, far_lo_ref, x_ref, g_ref, win_ref, cw_ref, wco_ref,
                 qt_ref, k_ref, vt_ref, gc_ref, sga_ref,
                 carry_ref, kmean_ref):
    tm = x_ref.shape[1]
    c, a, d = CONV_WIDTH, ATTN_WIDTH, D_MODEL
    s = pl.program_id(1)
    blocks_per_tile = tm // MOBA_BLOCK
    blk0 = s * blocks_per_tile

    @pl.when(s == 0)
    def _():
        carry_ref[...] = jnp.zeros_like(carry_ref)
        kmean_ref[...] = jnp.zeros_like(kmean_ref)

    h = _rms(x_ref[0], g_ref[...]).astype(BF16)

    def seg(lo, width):
        return jnp.dot(h, win_ref[:, lo:lo + width], preferred_element_type=F32)

    u = seg(2 * c, c) * seg(0, c)
    row = lax.broadcasted_iota(jnp.int32, u.shape, 0)
    prev = carry_ref[...]
    p1, p2 = prev[7:8], prev[6:7]
    u1 = jnp.where(row == 0, p1, pltpu.roll(u, 1, 0))
    u2 = jnp.where(row == 0, p2, jnp.where(row == 1, p1, pltpu.roll(u, 2, 0)))
    carry_ref[...] = u[tm - 8:]
    cw = cw_ref[...]
    conv = cw[2:3] * u + cw[1:2] * u1 + cw[0:1] * u2
    z = (seg(c, c) * conv).astype(BF16)
    yc = jnp.dot(z, wco_ref[...], preferred_element_type=F32)
    gc_ref[0] = jax.nn.sigmoid(seg(3 * c + 3 * a, d)) * yc
    sga_ref[0] = jax.nn.sigmoid(seg(3 * c + 3 * a + d, d))

    q = seg(3 * c, a)
    k = seg(3 * c + a, a)
    v = seg(3 * c + 2 * a, a)

    rows_nb = lax.broadcasted_iota(jnp.int32, kmean_ref.shape, 0)
    km = kmean_ref[...]
    for bi in range(blocks_per_tile):
        ks = jnp.sum(k[bi * MOBA_BLOCK:(bi + 1) * MOBA_BLOCK], axis=0, keepdims=True) * (1.0 / MOBA_BLOCK)
        km = jnp.where(rows_nb == blk0 + bi, ks, km)
    kmean_ref[...] = km

    qt = q.T
    vt = v.T
    n_iota = lax.broadcasted_iota(jnp.int32, (MASK_SLOTS, tm), 0)
    tok_blk = blk0 + lax.broadcasted_iota(jnp.int32, (MASK_SLOTS, tm), 1) // MOBA_BLOCK
    past = n_iota < tok_blk
    far = tok_blk - n_iota >= FAR_DIST
    ones_row = jnp.where(lax.broadcasted_iota(jnp.int32, (V_ROWS - HEAD_DIM, tm), 0) == 0, 1.0, 0.0)
    for hd in range(N_HEADS):
        qth = qt[hd * HEAD_DIM:(hd + 1) * HEAD_DIM]
        gate = jnp.dot(km[:, hd * HEAD_DIM:(hd + 1) * HEAD_DIM], qth,
                       precision=lax.Precision.HIGHEST, preferred_element_type=F32)
        g = jnp.where(past, gate, NEG)
        sel = n_iota == tok_blk
        for _ in range(MOBA_TOPK):
            m = jnp.max(g, axis=0, keepdims=True)
            idx = jnp.min(jnp.where(g == m, n_iota, MASK_SLOTS), axis=0, keepdims=True)
            pick = n_iota == idx
            sel = sel | (pick & past)
            g = jnp.where(pick, -jnp.inf, g)
        mask_hi = jnp.where(sel, jnp.where(far, far_hi_ref[hd], 0.0), NEG)
        mask_lo = jnp.where(sel & far, far_lo_ref[hd], 0.0)
        qt_ref[0, hd] = jnp.concatenate(
            [qth * (HEAD_DIM ** -0.5 * LOG2E), mask_hi, mask_lo], axis=0).astype(BF16)
        vt_ref[0, hd] = jnp.concatenate(
            [vt[hd * HEAD_DIM:(hd + 1) * HEAD_DIM], ones_row], axis=0).astype(BF16)

    lane = lax.broadcasted_iota(jnp.int32, (tm, LANES), 1)
    row_blk = blk0 + lax.broadcasted_iota(jnp.int32, (tm, LANES), 0) // MOBA_BLOCK
    k_pat = jnp.where(lane % MASK_SLOTS == row_blk, 1.0, 0.0)
    low = lane < HEAD_DIM
    for hp in range(N_HEADS // 2):
        kt = k[:, hp * LANES:(hp + 1) * LANES]
        k_ref[0, 2 * hp] = jnp.where(low, kt, k_pat).astype(BF16)
        k_ref[0, 2 * hp + 1] = jnp.where(low, pltpu.roll(kt, HEAD_DIM, 1), k_pat).astype(BF16)


def _projection(x, g, w_in, conv_w, w_conv_out, far_hi, far_lo):
    b, s, d = x.shape
    tm = TOKEN_TILE
    const = lambda shape: pl.BlockSpec(shape, lambda bi, si: (0,) * len(shape),
                                       pipeline_mode=pl.Buffered(1))
    smem = pl.BlockSpec(memory_space=pltpu.SMEM)
    tok = pl.BlockSpec((1, tm, d), lambda bi, si: (bi, si, 0))
    return pl.pallas_call(
        _proj_kernel,
        grid=(b, s // tm),
        in_specs=[smem, smem, tok, const((1, d)), const((d, N_IN)), const((CONV_K, CONV_WIDTH)),
                  const((CONV_WIDTH, d))],
        out_specs=[pl.BlockSpec((1, N_HEADS, AUG, tm), lambda bi, si: (bi, 0, 0, si)),
                   pl.BlockSpec((1, N_HEADS, tm, AUG), lambda bi, si: (bi, 0, si, 0)),
                   pl.BlockSpec((1, N_HEADS, V_ROWS, tm), lambda bi, si: (bi, 0, 0, si)),
                   tok, tok],
        out_shape=[jax.ShapeDtypeStruct((b, N_HEADS, AUG, s), BF16),
                   jax.ShapeDtypeStruct((b, N_HEADS, s, AUG), BF16),
                   jax.ShapeDtypeStruct((b, N_HEADS, V_ROWS, s), BF16),
                   jax.ShapeDtypeStruct((b, s, d), F32),
                   jax.ShapeDtypeStruct((b, s, d), F32)],
        scratch_shapes=[pltpu.VMEM((8, CONV_WIDTH), F32),
                        pltpu.VMEM((MASK_SLOTS, ATTN_WIDTH), F32)],
        compiler_params=pltpu.CompilerParams(
            dimension_semantics=("arbitrary", "arbitrary"), vmem_limit_bytes=VMEM_LIMIT),
        name="moba_projection",
    )(far_hi, far_lo, x, g, w_in, conv_w, w_conv_out)


def _attn_kernel(qt_ref, k_ref, vt_ref, bias_ref, o_ref, s_ref, tmax_ref, m_ref, acc_ref):
    qi = pl.program_id(2)
    n_tiles = pl.num_programs(2)
    n_heads = qt_ref.shape[1]

    def produce(t, slot):
        t = jnp.minimum(t, n_tiles - 1)
        keys = pl.ds(pl.multiple_of(t * SUPER, SUPER), SUPER)
        delta = jnp.clip(qi - t, 0, N_NEAR)
        for hh in range(n_heads):
            st = jnp.dot(k_ref[0, hh, keys, :], qt_ref[0, hh], preferred_element_type=F32)
            st = st + bias_ref[hh, delta]
            s_ref[slot, hh] = st
            tmax_ref[slot, hh] = jnp.max(st, axis=0, keepdims=True)

    def consume(t, slot):
        keys = pl.ds(pl.multiple_of(t * SUPER, SUPER), SUPER)
        for hh in range(n_heads):
            m = m_ref[hh]
            m_new = jnp.maximum(m, tmax_ref[slot, hh])
            p = jnp.exp2(s_ref[slot, hh] - m_new).astype(BF16)
            pv = jnp.dot(vt_ref[0, hh, :, keys], p, preferred_element_type=F32)
            acc_ref[hh] = jnp.exp2(m - m_new) * acc_ref[hh] + pv
            m_ref[hh] = m_new

    m_ref[...] = jnp.full(m_ref.shape, -jnp.inf, F32)
    acc_ref[...] = jnp.zeros(acc_ref.shape, F32)
    produce(0, 0)

    def pair(jj, _):
        produce(2 * jj + 1, 1)
        consume(2 * jj, 0)
        produce(2 * jj + 2, 0)
        consume(2 * jj + 1, 1)
        return _

    n_consume = qi + 1
    lax.fori_loop(0, n_consume // 2, pair, 0)

    @pl.when(n_consume % 2 == 1)
    def _():
        consume(qi, 0)

    outs = [(acc_ref[hh, :HEAD_DIM] / acc_ref[hh, HEAD_DIM:HEAD_DIM + 1]).T for hh in range(n_heads)]
    o_ref[0] = jnp.concatenate(outs, axis=-1).astype(o_ref.dtype)


def _attention(qt, k_aug, vt, bias):
    b, _, _, s = qt.shape
    hb = 2
    return pl.pallas_call(
        _attn_kernel,
        grid=(N_HEADS // hb, b, s // SUPER),
        in_specs=[pl.BlockSpec((1, hb, AUG, SUPER), lambda hp, bi, i: (bi, hp, 0, i)),
                  pl.BlockSpec((1, hb, s, AUG), lambda hp, bi, i: (bi, hp, 0, 0)),
                  pl.BlockSpec((1, hb, V_ROWS, s), lambda hp, bi, i: (bi, hp, 0, 0)),
                  pl.BlockSpec((hb, N_NEAR + 1, SUPER, SUPER), lambda hp, bi, i: (hp, 0, 0, 0),
                               pipeline_mode=pl.Buffered(1))],
        out_specs=pl.BlockSpec((1, SUPER, hb * HEAD_DIM), lambda hp, bi, i: (bi, i, hp)),
        out_shape=jax.ShapeDtypeStruct((b, s, ATTN_WIDTH), BF16),
        scratch_shapes=[pltpu.VMEM((2, hb, SUPER, SUPER), F32),
                        pltpu.VMEM((2, hb, 1, SUPER), F32),
                        pltpu.VMEM((hb, 1, SUPER), F32),
                        pltpu.VMEM((hb, V_ROWS, SUPER), F32)],
        compiler_params=pltpu.CompilerParams(
            dimension_semantics=("arbitrary", "arbitrary", "arbitrary"),
            vmem_limit_bytes=VMEM_LIMIT),
        name="moba_attention",
    )(qt, k_aug, vt, bias)


def _out_kernel(x_ref, att_ref, gc_ref, sga_ref, wao_ref, wo_ref, gpost_ref, gpre_ref,
                w1_ref, w2_ref, gmpost_ref, o_ref):
    ya = jnp.dot(att_ref[0], wao_ref[...], preferred_element_type=F32)
    m = (gc_ref[0] + sga_ref[0] * ya).astype(BF16)
    mix = jnp.dot(m, wo_ref[...], preferred_element_type=F32)
    x1 = x_ref[0] + _rms(mix, gpost_ref[...])
    h2 = _rms(x1, gpre_ref[...]).astype(BF16)
    f = jnp.zeros(x1.shape, F32)
    for ci in range(D_FF // FF_CHUNK):
        cols = slice(ci * FF_CHUNK, (ci + 1) * FF_CHUNK)
        act = jnp.dot(h2, w1_ref[:, cols], preferred_element_type=F32)
        act = jnp.square(jnp.maximum(act, 0.0)).astype(BF16)
        f = f + jnp.dot(act, w2_ref[cols, :], preferred_element_type=F32)
    o_ref[0] = x1 + _rms(f, gmpost_ref[...])


def _output(x, att, gc, sga, w_attn_out, w_o, g_post, g_mlp_pre, w1, w2, g_mlp_post):
    b, s, d = x.shape
    tm = TOKEN_TILE
    const = lambda shape: pl.BlockSpec(shape, lambda bi, si: (0,) * len(shape),
                                       pipeline_mode=pl.Buffered(1))
    tok = pl.BlockSpec((1, tm, d), lambda bi, si: (bi, si, 0))
    return pl.pallas_call(
        _out_kernel,
        grid=(b, s // tm),
        in_specs=[tok, pl.BlockSpec((1, tm, ATTN_WIDTH), lambda bi, si: (bi, si, 0)), tok, tok,
                  const((ATTN_WIDTH, d)), const((d, d)), const((1, d)), const((1, d)),
                  const((d, D_FF)), const((D_FF, d)), const((1, d))],
        out_specs=tok,
        out_shape=jax.ShapeDtypeStruct((b, s, d), F32),
        compiler_params=pltpu.CompilerParams(
            dimension_semantics=("arbitrary", "arbitrary"), vmem_limit_bytes=VMEM_LIMIT),
        name="moba_output_mlp",
    )(x, att, gc, sga, w_attn_out, w_o, g_post, g_mlp_pre, w1, w2, g_mlp_post)


def kernel(x, ln_mix_pre, w_in, conv_w, w_conv_out, w_attn_out, rel_bias, w_o, ln_mix_post,
           ln_mlp_pre, w_mlp_in, w_mlp_out, ln_mlp_post):
    depth = w_in.shape[0]
    b, s, d = x.shape
    assert d == D_MODEL and s % TOKEN_TILE == 0 and TOKEN_TILE % MOBA_BLOCK == 0 and s % SUPER == 0
    assert MOBA_TOPK <= s // MOBA_BLOCK <= MASK_SLOTS
    bias = _bias_tiles(rel_bias)
    far_hi, far_lo = _far_bias(rel_bias)
    for l in range(depth):
        qt, k_aug, vt, gc, sga = _projection(
            x, ln_mix_pre[l][None], w_in[l].astype(BF16), conv_w[l], w_conv_out[l].astype(BF16),
            far_hi, far_lo)
        att = _attention(qt, k_aug, vt, bias)
        x = _output(x, att, gc, sga, w_attn_out[l].astype(BF16), w_o[l].astype(BF16),
                    ln_mix_post[l][None], ln_mlp_pre[l][None], w_mlp_in[l].astype(BF16),
                    w_mlp_out[l].astype(BF16), ln_mlp_post[l][None])
    return x
```

```python
import functools
import math

import jax
import jax.numpy as jnp
from jax import lax
from jax.experimental import pallas as pl
from jax.experimental.pallas import tpu as pltpu

D_MODEL = 1024
CONV_WIDTH = D_MODEL // 2
CONV_K = 3
N_HEADS = 8
HEAD_DIM = 64
ATTN_WIDTH = N_HEADS * HEAD_DIM
MOBA_BLOCK = 256
MOBA_TOPK = 3
N_BUCKETS = 32
MAX_DISTANCE = 1024
D_FF = 4 * D_MODEL
RMS_EPS = 1e-6
N_IN = 3 * CONV_WIDTH + 3 * ATTN_WIDTH + 2 * D_MODEL
NEG = -1e30
LOG2E = math.log2(math.e)

LANES = 128
BF16_SUBLANES = 16
AUG = 2 * HEAD_DIM
MASK_SLOTS = (AUG - HEAD_DIM) // 2
V_ROWS = HEAD_DIM + BF16_SUBLANES
SUPER = 2 * MOBA_BLOCK
FAR_DIST = -(-(MAX_DISTANCE + MOBA_BLOCK) // MOBA_BLOCK)
N_NEAR = (FAR_DIST + 1) // 2
TOKEN_TILE = 512
FF_CHUNK = 1024
VMEM_LIMIT = 56 * 1024 * 1024

F32 = jnp.float32
BF16 = jnp.bfloat16


def _rms(x, g):
    return (x * lax.rsqrt(jnp.mean(x * x, axis=-1, keepdims=True) + RMS_EPS)) * g


def _t5_causal_bucket(rel):
    n = jnp.maximum(rel, 0)
    max_exact = N_BUCKETS // 2
    nf = jnp.maximum(n, max_exact).astype(F32)
    large = max_exact + (jnp.log(nf / max_exact) / math.log(MAX_DISTANCE / max_exact)
                         * (N_BUCKETS - max_exact)).astype(jnp.int32)
    large = jnp.minimum(large, N_BUCKETS - 1)
    return jnp.where(n < max_exact, n, large)


def _bias_tiles(rel_bias):
    h = rel_bias.shape[0]
    n = N_NEAR + 1
    m = jnp.arange(2 * SUPER)
    delta = jnp.arange(n)[:, None]
    rel = delta * SUPER + jnp.where(m < SUPER, m, m - 2 * SUPER)
    w = jnp.where(rel >= 0, rel_bias[:, _t5_causal_bucket(rel)] * LOG2E, NEG).astype(F32)
    t = jnp.broadcast_to(w[:, :, None, :], (h, n, SUPER, 2 * SUPER)).reshape(h, n, -1)
    t = t[:, :, :SUPER * (2 * SUPER - 1)].reshape(h, n, SUPER, 2 * SUPER - 1)[..., :SUPER]
    kb = jnp.arange(SUPER)[:, None] // MOBA_BLOCK
    qb = jnp.arange(SUPER)[None, :] // MOBA_BLOCK
    dist = (SUPER // MOBA_BLOCK) * delta[:, :, None] + qb - kb
    return jnp.where(dist >= FAR_DIST, 0.0, t)


def _far_bias(rel_bias):
    c = rel_bias[:, _t5_causal_bucket(jnp.int32(FAR_DIST * MOBA_BLOCK))] * LOG2E
    hi = c.astype(BF16).astype(F32)
    return hi, c - hi


def _proj_kernel(far_hi_ref, far_lo_ref, x_ref, g_ref, win_ref, cw_ref, wco_ref,
                 qt_ref, k_ref, vt_ref, gc_ref, sga_ref,
                 carry_ref, kmean_ref):
    tm = x_ref.shape[1]
    c, a, d = CONV_WIDTH, ATTN_WIDTH, D_MODEL
    s = pl.program_id(1)
    blocks_per_tile = tm // MOBA_BLOCK
    blk0 = s * blocks_per_tile

    @pl.when(s == 0)
    def _():
        carry_ref[...] = jnp.zeros_like(carry_ref)
        kmean_ref[...] = jnp.zeros_like(kmean_ref)

    h = _rms(x_ref[0], g_ref[...]).astype(BF16)

    def seg(lo, width):
        return jnp.dot(h, win_ref[:, lo:lo + width], preferred_element_type=F32)

    u = seg(2 * c, c) * seg(0, c)
    row = lax.broadcasted_iota(jnp.int32, u.shape, 0)
    prev = carry_ref[...]
    p1, p2 = prev[7:8], prev[6:7]
    u1 = jnp.where(row == 0, p1, pltpu.roll(u, 1, 0))
    u2 = jnp.where(row == 0, p2, jnp.where(row == 1, p1, pltpu.roll(u, 2, 0)))
    carry_ref[...] = u[tm - 8:]
    cw = cw_ref[...]
    conv = cw[2:3] * u + cw[1:2] * u1 + cw[0:1] * u2
    z = (seg(c, c) * conv).astype(BF16)
    yc = jnp.dot(z, wco_ref[...], preferred_element_type=F32)
    gc_ref[0] = jax.nn.sigmoid(seg(3 * c + 3 * a, d)) * yc
    sga_ref[0] = jax.nn.sigmoid(seg(3 * c + 3 * a + d, d))

    q = seg(3 * c, a)
    k = seg(3 * c + a, a)
    v = seg(3 * c + 2 * a, a)

    rows_nb = lax.broadcasted_iota(jnp.int32, kmean_ref.shape, 0)
    km = kmean_ref[...]
    for bi in range(blocks_per_tile):
        ks = jnp.sum(k[bi * MOBA_BLOCK:(bi + 1) * MOBA_BLOCK], axis=0, keepdims=True) * (1.0 / MOBA_BLOCK)
        km = jnp.where(rows_nb == blk0 + bi, ks, km)
    kmean_ref[...] = km

    qt = q.T
    vt = v.T
    n_iota = lax.broadcasted_iota(jnp.int32, (MASK_SLOTS, tm), 0)
    tok_blk = blk0 + lax.broadcasted_iota(jnp.int32, (MASK_SLOTS, tm), 1) // MOBA_BLOCK
    past = n_iota < tok_blk
    far = tok_blk - n_iota >= FAR_DIST
    ones_row = jnp.where(lax.broadcasted_iota(jnp.int32, (V_ROWS - HEAD_DIM, tm), 0) == 0, 1.0, 0.0)
    for hd in range(N_HEADS):
        qth = qt[hd * HEAD_DIM:(hd + 1) * HEAD_DIM]
        gate = jnp.dot(km[:, hd * HEAD_DIM:(hd + 1) * HEAD_DIM], qth,
                       precision=lax.Precision.HIGHEST, preferred_element_type=F32)
        g = jnp.where(past, gate, NEG)
        sel = n_iota == tok_blk
        for _ in range(MOBA_TOPK):
            m = jnp.max(g, axis=0, keepdims=True)
            idx = jnp.min(jnp.where(g == m, n_iota, MASK_SLOTS), axis=0, keepdims=True)
            pick = n_iota == idx
            sel = sel | (pick & past)
            g = jnp.where(pick, -jnp.inf, g)
        mask_hi = jnp.where(sel, jnp.where(far, far_hi_ref[hd], 0.0), NEG)
        mask_lo = jnp.where(sel & far, far_lo_ref[hd], 0.0)
        qt_ref[0, hd] = jnp.concatenate(
            [qth * (HEAD_DIM ** -0.5 * LOG2E), mask_hi, mask_lo], axis=0).astype(BF16)
        vt_ref[0, hd] = jnp.concatenate(
            [vt[hd * HEAD_DIM:(hd + 1) * HEAD_DIM], ones_row], axis=0).astype(BF16)

    lane = lax.broadcasted_iota(jnp.int32, (tm, LANES), 1)
    row_blk = blk0 + lax.broadcasted_iota(jnp.int32, (tm, LANES), 0) // MOBA_BLOCK
    k_pat = jnp.where(lane % MASK_SLOTS == row_blk, 1.0, 0.0)
    low = lane < HEAD_DIM
    for hp in range(N_HEADS // 2):
        kt = k[:, hp * LANES:(hp + 1) * LANES]
        k_ref[0, 2 * hp] = jnp.where(low, kt, k_pat).astype(BF16)
        k_ref[0, 2 * hp + 1] = jnp.where(low, pltpu.roll(kt, HEAD_DIM, 1), k_pat).astype(BF16)


def _projection(x, g, w_in, conv_w, w_conv_out, far_hi, far_lo):
    b, s, d = x.shape
    tm = TOKEN_TILE
    const = lambda shape: pl.BlockSpec(shape, lambda bi, si: (0,) * len(shape),
                                       pipeline_mode=pl.Buffered(1))
    smem = pl.BlockSpec(memory_space=pltpu.SMEM)
    tok = pl.BlockSpec((1, tm, d), lambda bi, si: (bi, si, 0))
    return pl.pallas_call(
        _proj_kernel,
        grid=(b, s // tm),
        in_specs=[smem, smem, tok, const((1, d)), const((d, N_IN)), const((CONV_K, CONV_WIDTH)),
                  const((CONV_WIDTH, d))],
        out_specs=[pl.BlockSpec((1, N_HEADS, AUG, tm), lambda bi, si: (bi, 0, 0, si)),
                   pl.BlockSpec((1, N_HEADS, tm, AUG), lambda bi, si: (bi, 0, si, 0)),
                   pl.BlockSpec((1, N_HEADS, V_ROWS, tm), lambda bi, si: (bi, 0, 0, si)),
                   tok, tok],
        out_shape=[jax.ShapeDtypeStruct((b, N_HEADS, AUG, s), BF16),
                   jax.ShapeDtypeStruct((b, N_HEADS, s, AUG), BF16),
                   jax.ShapeDtypeStruct((b, N_HEADS, V_ROWS, s), BF16),
                   jax.ShapeDtypeStruct((b, s, d), F32),
                   jax.ShapeDtypeStruct((b, s, d), F32)],
        scratch_shapes=[pltpu.VMEM((8, CONV_WIDTH), F32),
                        pltpu.VMEM((MASK_SLOTS, ATTN_WIDTH), F32)],
        compiler_params=pltpu.CompilerParams(
            dimension_semantics=("arbitrary", "arbitrary"), vmem_limit_bytes=VMEM_LIMIT),
        name="moba_projection",
    )(far_hi, far_lo, x, g, w_in, conv_w, w_conv_out)


def _attn_kernel(qa_ref, qb_ref, k_ref, vt_ref, bias_ref, oa_ref, ob_ref,
                 q_ref, s_ref, tmax_ref, m_ref, acc_ref, *, n_tiles):
    pr = pl.program_id(2)
    n_items = n_tiles + 1
    n_far = n_items - 2 * N_NEAR
    n_heads = qa_ref.shape[1]
    tile_a, tile_b = pr, n_tiles - 1 - pr

    def item(w):
        far_b = jnp.minimum(tile_b + 1 - N_NEAR, n_far)
        rest_b = tile_b + 1 - far_b
        u = w - n_far
        is_far = w < n_far
        second = jnp.where(is_far, w < far_b, u < rest_b)
        t = jnp.where(second, jnp.where(is_far, w, far_b + u),
                      jnp.where(is_far, w - far_b, n_far - far_b + u - rest_b))
        return second.astype(jnp.int32), jnp.where(second, tile_b, tile_a), t

    def produce(w, slot, near):
        second, qi, t = item(w)
        keys = pl.ds(pl.multiple_of(t * SUPER, SUPER), SUPER)
        for hh in range(n_heads):
            st = jnp.dot(k_ref[0, hh, keys, :], q_ref[second, hh], preferred_element_type=F32)
            if near:
                st = st + bias_ref[hh, jnp.minimum(qi - t, N_NEAR)]
            s_ref[slot, hh] = st
            tmax_ref[slot, hh] = jnp.max(st, axis=0, keepdims=True)

    def consume(w, slot):
        second, _, t = item(w)
        keys = pl.ds(pl.multiple_of(t * SUPER, SUPER), SUPER)
        for hh in range(n_heads):
            m = m_ref[second, hh]
            m_new = jnp.maximum(m, tmax_ref[slot, hh])
            p = jnp.exp2(s_ref[slot, hh] - m_new).astype(BF16)
            pv = jnp.dot(vt_ref[0, hh, :, keys], p, preferred_element_type=F32)
            acc_ref[second, hh] = jnp.exp2(m - m_new) * acc_ref[second, hh] + pv
            m_ref[second, hh] = m_new

    q_ref[0] = qa_ref[0]
    q_ref[1] = qb_ref[0]
    m_ref[...] = jnp.full(m_ref.shape, -jnp.inf, F32)
    acc_ref[...] = jnp.zeros(acc_ref.shape, F32)
    produce(0, 0, near=False)

    def pair(jj, _, near):
        produce(2 * jj + 1, 1, near)
        consume(2 * jj, 0)
        produce(2 * jj + 2, 0, near)
        consume(2 * jj + 1, 1)
        return _

    far_pairs = (n_far - 1) // 2
    lax.fori_loop(0, far_pairs, functools.partial(pair, near=False), 0, unroll=2)
    lax.fori_loop(far_pairs, n_items // 2, functools.partial(pair, near=True), 0, unroll=2)
    consume(n_items - 1, 0)

    for second, o_ref in enumerate((oa_ref, ob_ref)):
        outs = [(acc_ref[second, hh, :HEAD_DIM] / acc_ref[second, hh, HEAD_DIM:HEAD_DIM + 1]).T
                for hh in range(n_heads)]
        o_ref[0] = jnp.concatenate(outs, axis=-1).astype(o_ref.dtype)


def _attention(qt, k_aug, vt, bias):
    b, _, _, s = qt.shape
    hb = 2
    n_tiles = s // SUPER
    half = n_tiles // 2
    q_spec = lambda tile: pl.BlockSpec((1, hb, AUG, SUPER), lambda hp, bi, pr: (bi, hp, 0, tile(pr)))
    o_spec = lambda tile: pl.BlockSpec((1, SUPER, hb * HEAD_DIM), lambda hp, bi, pr: (bi, tile(pr), hp))
    o_shape = jax.ShapeDtypeStruct((b, s // 2, ATTN_WIDTH), BF16)
    return pl.pallas_call(
        functools.partial(_attn_kernel, n_tiles=n_tiles),
        grid=(N_HEADS // hb, b, half),
        in_specs=[q_spec(lambda pr: pr), q_spec(lambda pr: n_tiles - 1 - pr),
                  pl.BlockSpec((1, hb, s, AUG), lambda hp, bi, pr: (bi, hp, 0, 0)),
                  pl.BlockSpec((1, hb, V_ROWS, s), lambda hp, bi, pr: (bi, hp, 0, 0)),
                  pl.BlockSpec((hb, N_NEAR + 1, SUPER, SUPER), lambda hp, bi, pr: (hp, 0, 0, 0),
                               pipeline_mode=pl.Buffered(1))],
        out_specs=[o_spec(lambda pr: pr), o_spec(lambda pr: half - 1 - pr)],
        out_shape=[o_shape, o_shape],
        scratch_shapes=[pltpu.VMEM((2, hb, AUG, SUPER), BF16),
                        pltpu.VMEM((2, hb, SUPER, SUPER), F32),
                        pltpu.VMEM((2, hb, 1, SUPER), F32),
                        pltpu.VMEM((2, hb, 1, SUPER), F32),
                        pltpu.VMEM((2, hb, V_ROWS, SUPER), F32)],
        compiler_params=pltpu.CompilerParams(
            dimension_semantics=("arbitrary", "arbitrary", "arbitrary"),
            vmem_limit_bytes=VMEM_LIMIT),
        name="moba_attention",
    )(qt, qt, k_aug, vt, bias)


def _out_kernel(x_ref, att_lo_ref, att_hi_ref, gc_ref, sga_ref, wao_ref, wo_ref, gpost_ref, gpre_ref,
                w1_ref, w2_ref, gmpost_ref, o_ref):
    first_half = pl.program_id(1) < pl.num_programs(1) // 2
    att = jnp.where(first_half, att_lo_ref[0], att_hi_ref[0])
    ya = jnp.dot(att, wao_ref[...], preferred_element_type=F32)
    m = (gc_ref[0] + sga_ref[0] * ya).astype(BF16)
    mix = jnp.dot(m, wo_ref[...], preferred_element_type=F32)
    x1 = x_ref[0] + _rms(mix, gpost_ref[...])
    h2 = _rms(x1, gpre_ref[...]).astype(BF16)
    f = jnp.zeros(x1.shape, F32)
    for ci in range(D_FF // FF_CHUNK):
        cols = slice(ci * FF_CHUNK, (ci + 1) * FF_CHUNK)
        act = jnp.dot(h2, w1_ref[:, cols], preferred_element_type=F32)
        act = jnp.square(jnp.maximum(act, 0.0)).astype(BF16)
        f = f + jnp.dot(act, w2_ref[cols, :], preferred_element_type=F32)
    o_ref[0] = x1 + _rms(f, gmpost_ref[...])


def _output(x, att_lo, att_hi, gc, sga, w_attn_out, w_o, g_post, g_mlp_pre, w1, w2, g_mlp_post):
    b, s, d = x.shape
    tm = TOKEN_TILE
    half = s // tm // 2
    const = lambda shape: pl.BlockSpec(shape, lambda bi, si: (0,) * len(shape),
                                       pipeline_mode=pl.Buffered(1))
    tok = pl.BlockSpec((1, tm, d), lambda bi, si: (bi, si, 0))
    return pl.pallas_call(
        _out_kernel,
        grid=(b, s // tm),
        in_specs=[tok,
                  pl.BlockSpec((1, tm, ATTN_WIDTH), lambda bi, si: (bi, jnp.minimum(si, half - 1), 0)),
                  pl.BlockSpec((1, tm, ATTN_WIDTH), lambda bi, si: (bi, jnp.maximum(si - half, 0), 0)),
                  tok, tok,
                  const((ATTN_WIDTH, d)), const((d, d)), const((1, d)), const((1, d)),
                  const((d, D_FF)), const((D_FF, d)), const((1, d))],
        out_specs=tok,
        out_shape=jax.ShapeDtypeStruct((b, s, d), F32),
        compiler_params=pltpu.CompilerParams(
            dimension_semantics=("arbitrary", "arbitrary"), vmem_limit_bytes=VMEM_LIMIT),
        name="moba_output_mlp",
    )(x, att_lo, att_hi, gc, sga, w_attn_out, w_o, g_post, g_mlp_pre, w1, w2, g_mlp_post)


def kernel(x, ln_mix_pre, w_in, conv_w, w_conv_out, w_attn_out, rel_bias, w_o, ln_mix_post,
           ln_mlp_pre, w_mlp_in, w_mlp_out, ln_mlp_post):
    depth = w_in.shape[0]
    b, s, d = x.shape
    assert d == D_MODEL and TOKEN_TILE == SUPER and s % (2 * SUPER) == 0 and s // SUPER >= 2 * N_NEAR
    assert MOBA_TOPK <= s // MOBA_BLOCK <= MASK_SLOTS
    bias = _bias_tiles(rel_bias)
    far_hi, far_lo = _far_bias(rel_bias)
    for l in range(depth):
        qt, k_aug, vt, gc, sga = _projection(
            x, ln_mix_pre[l][None], w_in[l].astype(BF16), conv_w[l], w_conv_out[l].astype(BF16),
            far_hi, far_lo)
        att_lo, att_hi = _attention(qt, k_aug, vt, bias)
        x = _output(x, att_lo, att_hi, gc, sga, w_attn_out[l].astype(BF16), w_o[l].astype(BF16),
                    ln_mix_post[l][None], ln_mlp_pre[l][None], w_mlp_in[l].astype(BF16),
                    w_mlp_out[l].astype(BF16), ln_mlp_post[l][None])
    return x
```

```python
import functools
import math

import jax
import jax.numpy as jnp
from jax import lax
from jax.experimental import pallas as pl
from jax.experimental.pallas import tpu as pltpu

D_MODEL = 1024
CONV_WIDTH = D_MODEL // 2
CONV_K = 3
N_HEADS = 8
HEAD_DIM = 64
ATTN_WIDTH = N_HEADS * HEAD_DIM
MOBA_BLOCK = 256
MOBA_TOPK = 3
N_BUCKETS = 32
MAX_DISTANCE = 1024
D_FF = 4 * D_MODEL
RMS_EPS = 1e-6
N_IN = 3 * CONV_WIDTH + 3 * ATTN_WIDTH + 2 * D_MODEL
NEG = -1e30
LOG2E = math.log2(math.e)

LANES = 128
BF16_SUBLANES = 16
AUG = 2 * HEAD_DIM
MASK_SLOTS = (AUG - HEAD_DIM) // 2
V_ROWS = HEAD_DIM + BF16_SUBLANES
SUPER = 2 * MOBA_BLOCK
FAR_DIST = -(-(MAX_DISTANCE + MOBA_BLOCK) // MOBA_BLOCK)
N_NEAR = (FAR_DIST + 1) // 2
TOKEN_TILE = 512
FF_CHUNK = 1024
VMEM_LIMIT = 56 * 1024 * 1024

F32 = jnp.float32
BF16 = jnp.bfloat16


def _rms(x, g):
    return (x * lax.rsqrt(jnp.mean(x * x, axis=-1, keepdims=True) + RMS_EPS)) * g


def _t5_causal_bucket(rel):
    n = jnp.maximum(rel, 0)
    max_exact = N_BUCKETS // 2
    nf = jnp.maximum(n, max_exact).astype(F32)
    large = max_exact + (jnp.log(nf / max_exact) / math.log(MAX_DISTANCE / max_exact)
                         * (N_BUCKETS - max_exact)).astype(jnp.int32)
    large = jnp.minimum(large, N_BUCKETS - 1)
    return jnp.where(n < max_exact, n, large)


def _bias_rows(rel_bias):
    m = jnp.arange(2 * SUPER)
    delta = jnp.arange(N_NEAR)[:, None]
    rel = delta * SUPER + jnp.where(m < SUPER, m, m - 2 * SUPER)
    w = jnp.where(rel >= 0, rel_bias[:, _t5_causal_bucket(rel)] * LOG2E, NEG).astype(F32)
    return w[:, :, None, :]


def _far_bias(rel_bias):
    c = rel_bias[:, _t5_causal_bucket(jnp.int32(FAR_DIST * MOBA_BLOCK))] * LOG2E
    hi = c.astype(BF16).astype(F32)
    return hi, c - hi


def _proj_kernel(far_hi_ref, far_lo_ref, x_ref, g_ref, win_ref, cw_ref, wco_ref,
                 qt_ref, k_ref, vt_ref, gc_ref, sga_ref,
                 carry_ref, kmean_ref):
    tm = x_ref.shape[1]
    c, a, d = CONV_WIDTH, ATTN_WIDTH, D_MODEL
    s = pl.program_id(1)
    blocks_per_tile = tm // MOBA_BLOCK
    blk0 = s * blocks_per_tile

    @pl.when(s == 0)
    def _():
        carry_ref[...] = jnp.zeros_like(carry_ref)
        kmean_ref[...] = jnp.zeros_like(kmean_ref)

    h = _rms(x_ref[0], g_ref[...]).astype(BF16)

    def seg(lo, width):
        return jnp.dot(h, win_ref[:, lo:lo + width], preferred_element_type=F32)

    u = seg(2 * c, c) * seg(0, c)
    row = lax.broadcasted_iota(jnp.int32, u.shape, 0)
    prev = carry_ref[...]
    p1, p2 = prev[7:8], prev[6:7]
    u1 = jnp.where(row == 0, p1, pltpu.roll(u, 1, 0))
    u2 = jnp.where(row == 0, p2, jnp.where(row == 1, p1, pltpu.roll(u, 2, 0)))
    carry_ref[...] = u[tm - 8:]
    cw = cw_ref[...]
    conv = cw[2:3] * u + cw[1:2] * u1 + cw[0:1] * u2
    z = (seg(c, c) * conv).astype(BF16)
    yc = jnp.dot(z, wco_ref[...], preferred_element_type=F32)
    gc_ref[0] = jax.nn.sigmoid(seg(3 * c + 3 * a, d)) * yc
    sga_ref[0] = jax.nn.sigmoid(seg(3 * c + 3 * a + d, d))

    q = seg(3 * c, a)
    k = seg(3 * c + a, a)
    v = seg(3 * c + 2 * a, a)

    rows_nb = lax.broadcasted_iota(jnp.int32, kmean_ref.shape, 0)
    km = kmean_ref[...]
    for bi in range(blocks_per_tile):
        ks = jnp.sum(k[bi * MOBA_BLOCK:(bi + 1) * MOBA_BLOCK], axis=0, keepdims=True) * (1.0 / MOBA_BLOCK)
        km = jnp.where(rows_nb == blk0 + bi, ks, km)
    kmean_ref[...] = km

    qt = q.T
    vt = v.T
    n_iota = lax.broadcasted_iota(jnp.int32, (MASK_SLOTS, tm), 0)
    tok_blk = blk0 + lax.broadcasted_iota(jnp.int32, (MASK_SLOTS, tm), 1) // MOBA_BLOCK
    past = n_iota < tok_blk
    far = tok_blk - n_iota >= FAR_DIST
    ones_row = jnp.where(lax.broadcasted_iota(jnp.int32, (V_ROWS - HEAD_DIM, tm), 0) == 0, 1.0, 0.0)
    for hd in range(N_HEADS):
        qth = qt[hd * HEAD_DIM:(hd + 1) * HEAD_DIM]
        gate = jnp.dot(km[:, hd * HEAD_DIM:(hd + 1) * HEAD_DIM], qth,
                       precision=lax.Precision.HIGHEST, preferred_element_type=F32)
        g = jnp.where(past, gate, NEG)
        sel = n_iota == tok_blk
        for _ in range(MOBA_TOPK):
            m = jnp.max(g, axis=0, keepdims=True)
            idx = jnp.min(jnp.where(g == m, n_iota, MASK_SLOTS), axis=0, keepdims=True)
            pick = n_iota == idx
            sel = sel | (pick & past)
            g = jnp.where(pick, -jnp.inf, g)
        mask_hi = jnp.where(sel, jnp.where(far, far_hi_ref[hd], 0.0), NEG)
        mask_lo = jnp.where(sel & far, far_lo_ref[hd], 0.0)
        qt_ref[0, hd] = jnp.concatenate(
            [qth * (HEAD_DIM ** -0.5 * LOG2E), mask_hi, mask_lo], axis=0).astype(BF16)
        vt_ref[0, hd] = jnp.concatenate(
            [vt[hd * HEAD_DIM:(hd + 1) * HEAD_DIM], ones_row], axis=0).astype(BF16)

    lane = lax.broadcasted_iota(jnp.int32, (tm, LANES), 1)
    row_blk = blk0 + lax.broadcasted_iota(jnp.int32, (tm, LANES), 0) // MOBA_BLOCK
    k_pat = jnp.where(lane % MASK_SLOTS == row_blk, 1.0, 0.0)
    low = lane < HEAD_DIM
    for hp in range(N_HEADS // 2):
        kt = k[:, hp * LANES:(hp + 1) * LANES]
        k_ref[0, 2 * hp] = jnp.where(low, kt, k_pat).astype(BF16)
        k_ref[0, 2 * hp + 1] = jnp.where(low, pltpu.roll(kt, HEAD_DIM, 1), k_pat).astype(BF16)


def _projection(x, g, w_in, conv_w, w_conv_out, far_hi, far_lo):
    b, s, d = x.shape
    tm = TOKEN_TILE
    const = lambda shape: pl.BlockSpec(shape, lambda bi, si: (0,) * len(shape),
                                       pipeline_mode=pl.Buffered(1))
    smem = pl.BlockSpec(memory_space=pltpu.SMEM)
    tok = pl.BlockSpec((1, tm, d), lambda bi, si: (bi, si, 0))
    return pl.pallas_call(
        _proj_kernel,
        grid=(b, s // tm),
        in_specs=[smem, smem, tok, const((1, d)), const((d, N_IN)), const((CONV_K, CONV_WIDTH)),
                  const((CONV_WIDTH, d))],
        out_specs=[pl.BlockSpec((1, N_HEADS, AUG, tm), lambda bi, si: (bi, 0, 0, si)),
                   pl.BlockSpec((1, N_HEADS, tm, AUG), lambda bi, si: (bi, 0, si, 0)),
                   pl.BlockSpec((1, N_HEADS, V_ROWS, tm), lambda bi, si: (bi, 0, 0, si)),
                   tok, tok],
        out_shape=[jax.ShapeDtypeStruct((b, N_HEADS, AUG, s), BF16),
                   jax.ShapeDtypeStruct((b, N_HEADS, s, AUG), BF16),
                   jax.ShapeDtypeStruct((b, N_HEADS, V_ROWS, s), BF16),
                   jax.ShapeDtypeStruct((b, s, d), F32),
                   jax.ShapeDtypeStruct((b, s, d), F32)],
        scratch_shapes=[pltpu.VMEM((8, CONV_WIDTH), F32),
                        pltpu.VMEM((MASK_SLOTS, ATTN_WIDTH), F32)],
        compiler_params=pltpu.CompilerParams(
            dimension_semantics=("arbitrary", "arbitrary"), vmem_limit_bytes=VMEM_LIMIT),
        name="moba_projection",
    )(far_hi, far_lo, x, g, w_in, conv_w, w_conv_out)


def _attn_kernel(qa_ref, qb_ref, k_ref, vt_ref, brow_ref, oa_ref, ob_ref,
                 q_ref, s_ref, tmax_ref, m_ref, acc_ref, bias_ref, *, n_tiles):
    pr = pl.program_id(2)
    n_items = n_tiles + 1
    n_far = n_items - 2 * N_NEAR
    n_heads = qa_ref.shape[1]
    tile_a, tile_b = pr, n_tiles - 1 - pr

    @pl.when((pl.program_id(1) == 0) & (pr == 0))
    def _():
        kb = lax.broadcasted_iota(jnp.int32, (SUPER, SUPER), 0) // MOBA_BLOCK
        qb = lax.broadcasted_iota(jnp.int32, (SUPER, SUPER), 1) // MOBA_BLOCK
        for hh in range(n_heads):
            for delta in range(N_NEAR):
                rows = jnp.broadcast_to(brow_ref[hh, delta], (SUPER, 2 * SUPER))
                tile = pltpu.roll(rows, 0, 1, stride=1, stride_axis=0)[:, :SUPER]
                dist = (SUPER // MOBA_BLOCK) * delta + qb - kb
                bias_ref[hh, delta] = jnp.where(dist >= FAR_DIST, 0.0, tile)
            bias_ref[hh, N_NEAR] = jnp.zeros((SUPER, SUPER), F32)

    def item(w):
        far_b = jnp.minimum(tile_b + 1 - N_NEAR, n_far)
        rest_b = tile_b + 1 - far_b
        u = w - n_far
        is_far = w < n_far
        second = jnp.where(is_far, w < far_b, u < rest_b)
        t = jnp.where(second, jnp.where(is_far, w, far_b + u),
                      jnp.where(is_far, w - far_b, n_far - far_b + u - rest_b))
        return second.astype(jnp.int32), jnp.where(second, tile_b, tile_a), t

    def produce(w, slot, near):
        second, qi, t = item(w)
        keys = pl.ds(pl.multiple_of(t * SUPER, SUPER), SUPER)
        for hh in range(n_heads):
            st = jnp.dot(k_ref[0, hh, keys, :], q_ref[second, hh], preferred_element_type=F32)
            if near:
                st = st + bias_ref[hh, jnp.minimum(qi - t, N_NEAR)]
            s_ref[slot, hh] = st
            tmax_ref[slot, hh] = jnp.max(st, axis=0, keepdims=True)

    def consume(w, slot):
        second, _, t = item(w)
        keys = pl.ds(pl.multiple_of(t * SUPER, SUPER), SUPER)
        for hh in range(n_heads):
            m = m_ref[second, hh]
            m_new = jnp.maximum(m, tmax_ref[slot, hh])
            p = jnp.exp2(s_ref[slot, hh] - m_new).astype(BF16)
            pv = jnp.dot(vt_ref[0, hh, :, keys], p, preferred_element_type=F32)
            acc_ref[second, hh] = jnp.exp2(m - m_new) * acc_ref[second, hh] + pv
            m_ref[second, hh] = m_new

    q_ref[0] = qa_ref[0]
    q_ref[1] = qb_ref[0]
    m_ref[...] = jnp.full(m_ref.shape, -jnp.inf, F32)
    acc_ref[...] = jnp.zeros(acc_ref.shape, F32)
    produce(0, 0, near=False)

    def pair(jj, _, near):
        produce(2 * jj + 1, 1, near)
        consume(2 * jj, 0)
        produce(2 * jj + 2, 0, near)
        consume(2 * jj + 1, 1)
        return _

    far_pairs = (n_far - 1) // 2
    lax.fori_loop(0, far_pairs, functools.partial(pair, near=False), 0, unroll=2)
    lax.fori_loop(far_pairs, n_items // 2, functools.partial(pair, near=True), 0, unroll=2)
    consume(n_items - 1, 0)

    for second, o_ref in enumerate((oa_ref, ob_ref)):
        outs = [(acc_ref[second, hh, :HEAD_DIM] / acc_ref[second, hh, HEAD_DIM:HEAD_DIM + 1]).T
                for hh in range(n_heads)]
        o_ref[0] = jnp.concatenate(outs, axis=-1).astype(o_ref.dtype)


def _attention(qt, k_aug, vt, bias_rows):
    b, _, _, s = qt.shape
    hb = 2
    n_tiles = s // SUPER
    half = n_tiles // 2
    q_spec = lambda tile: pl.BlockSpec((1, hb, AUG, SUPER), lambda hp, bi, pr: (bi, hp, 0, tile(pr)))
    o_spec = lambda tile: pl.BlockSpec((1, SUPER, hb * HEAD_DIM), lambda hp, bi, pr: (bi, tile(pr), hp))
    o_shape = jax.ShapeDtypeStruct((b, s // 2, ATTN_WIDTH), BF16)
    return pl.pallas_call(
        functools.partial(_attn_kernel, n_tiles=n_tiles),
        grid=(N_HEADS // hb, b, half),
        in_specs=[q_spec(lambda pr: pr), q_spec(lambda pr: n_tiles - 1 - pr),
                  pl.BlockSpec((1, hb, s, AUG), lambda hp, bi, pr: (bi, hp, 0, 0)),
                  pl.BlockSpec((1, hb, V_ROWS, s), lambda hp, bi, pr: (bi, hp, 0, 0)),
                  pl.BlockSpec((hb, N_NEAR, 1, 2 * SUPER), lambda hp, bi, pr: (hp, 0, 0, 0))],
        out_specs=[o_spec(lambda pr: pr), o_spec(lambda pr: half - 1 - pr)],
        out_shape=[o_shape, o_shape],
        scratch_shapes=[pltpu.VMEM((2, hb, AUG, SUPER), BF16),
                        pltpu.VMEM((2, hb, SUPER, SUPER), F32),
                        pltpu.VMEM((2, hb, 1, SUPER), F32),
                        pltpu.VMEM((2, hb, 1, SUPER), F32),
                        pltpu.VMEM((2, hb, V_ROWS, SUPER), F32),
                        pltpu.VMEM((hb, N_NEAR + 1, SUPER, SUPER), F32)],
        compiler_params=pltpu.CompilerParams(
            dimension_semantics=("arbitrary", "arbitrary", "arbitrary"),
            vmem_limit_bytes=VMEM_LIMIT),
        name="moba_attention",
    )(qt, qt, k_aug, vt, bias_rows)


def _out_kernel(x_ref, att_lo_ref, att_hi_ref, gc_ref, sga_ref, wao_ref, wo_ref, gpost_ref, gpre_ref,
                w1_ref, w2_ref, gmpost_ref, o_ref):
    first_half = pl.program_id(1) < pl.num_programs(1) // 2
    att = jnp.where(first_half, att_lo_ref[0], att_hi_ref[0])
    ya = jnp.dot(att, wao_ref[...], preferred_element_type=F32)
    m = (gc_ref[0] + sga_ref[0] * ya).astype(BF16)
    mix = jnp.dot(m, wo_ref[...], preferred_element_type=F32)
    x1 = x_ref[0] + _rms(mix, gpost_ref[...])
    h2 = _rms(x1, gpre_ref[...]).astype(BF16)
    f = jnp.zeros(x1.shape, F32)
    for ci in range(D_FF // FF_CHUNK):
        cols = slice(ci * FF_CHUNK, (ci + 1) * FF_CHUNK)
        act = jnp.dot(h2, w1_ref[:, cols], preferred_element_type=F32)
        act = jnp.square(jnp.maximum(act, 0.0)).astype(BF16)
        f = f + jnp.dot(act, w2_ref[cols, :], preferred_element_type=F32)
    o_ref[0] = x1 + _rms(f, gmpost_ref[...])


def _output(x, att_lo, att_hi, gc, sga, w_attn_out, w_o, g_post, g_mlp_pre, w1, w2, g_mlp_post):
    b, s, d = x.shape
    tm = TOKEN_TILE
    half = s // tm // 2
    const = lambda shape: pl.BlockSpec(shape, lambda bi, si: (0,) * len(shape),
                                       pipeline_mode=pl.Buffered(1))
    tok = pl.BlockSpec((1, tm, d), lambda bi, si: (bi, si, 0))
    return pl.pallas_call(
        _out_kernel,
        grid=(b, s // tm),
        in_specs=[tok,
                  pl.BlockSpec((1, tm, ATTN_WIDTH), lambda bi, si: (bi, jnp.minimum(si, half - 1), 0)),
                  pl.BlockSpec((1, tm, ATTN_WIDTH), lambda bi, si: (bi, jnp.maximum(si - half, 0), 0)),
                  tok, tok,
                  const((ATTN_WIDTH, d)), const((d, d)), const((1, d)), const((1, d)),
                  const((d, D_FF)), const((D_FF, d)), const((1, d))],
        out_specs=tok,
        out_shape=jax.ShapeDtypeStruct((b, s, d), F32),
        compiler_params=pltpu.CompilerParams(
            dimension_semantics=("arbitrary", "arbitrary"), vmem_limit_bytes=VMEM_LIMIT),
        name="moba_output_mlp",
    )(x, att_lo, att_hi, gc, sga, w_attn_out, w_o, g_post, g_mlp_pre, w1, w2, g_mlp_post)


def kernel(x, ln_mix_pre, w_in, conv_w, w_conv_out, w_attn_out, rel_bias, w_o, ln_mix_post,
           ln_mlp_pre, w_mlp_in, w_mlp_out, ln_mlp_post):
    depth = w_in.shape[0]
    b, s, d = x.shape
    assert d == D_MODEL and TOKEN_TILE == SUPER and s % (2 * SUPER) == 0 and s // SUPER >= 2 * N_NEAR
    assert MOBA_TOPK <= s // MOBA_BLOCK <= MASK_SLOTS
    bias_rows = _bias_rows(rel_bias)
    far_hi, far_lo = _far_bias(rel_bias)
    for l in range(depth):
        qt, k_aug, vt, gc, sga = _projection(
            x, ln_mix_pre[l][None], w_in[l].astype(BF16), conv_w[l], w_conv_out[l].astype(BF16),
            far_hi, far_lo)
        att_lo, att_hi = _attention(qt, k_aug, vt, bias_rows)
        x = _output(x, att_lo, att_hi, gc, sga, w_attn_out[l].astype(BF16), w_o[l].astype(BF16),
                    ln_mix_post[l][None], ln_mlp_pre[l][None], w_mlp_in[l].astype(BF16),
                    w_mlp_out[l].astype(BF16), ln_mlp_post[l][None])
    return x
```

```python
import functools
import math

import jax
import jax.numpy as jnp
from jax import lax
from jax.experimental import pallas as pl
from jax.experimental.pallas import tpu as pltpu

D_MODEL = 1024
CONV_WIDTH = D_MODEL // 2
CONV_K = 3
N_HEADS = 8
HEAD_DIM = 64
ATTN_WIDTH = N_HEADS * HEAD_DIM
MOBA_BLOCK = 256
MOBA_TOPK = 3
N_BUCKETS = 32
MAX_DISTANCE = 1024
D_FF = 4 * D_MODEL
RMS_EPS = 1e-6
N_IN = 3 * CONV_WIDTH + 3 * ATTN_WIDTH + 2 * D_MODEL
NEG = -1e30
LOG2E = math.log2(math.e)

LANES = 128
BF16_SUBLANES = 16
AUG = 2 * HEAD_DIM
MASK_SLOTS = (AUG - HEAD_DIM) // 2
V_ROWS = HEAD_DIM + BF16_SUBLANES
SUPER = 2 * MOBA_BLOCK
FAR_DIST = -(-(MAX_DISTANCE + MOBA_BLOCK) // MOBA_BLOCK)
N_NEAR = (FAR_DIST + 1) // 2
TOKEN_TILE = 512
FF_CHUNK = 1024
VMEM_LIMIT = 56 * 1024 * 1024

F32 = jnp.float32
BF16 = jnp.bfloat16


def _rms(x, g):
    return (x * lax.rsqrt(jnp.mean(x * x, axis=-1, keepdims=True) + RMS_EPS)) * g


def _t5_causal_bucket(rel):
    n = jnp.maximum(rel, 0)
    max_exact = N_BUCKETS // 2
    nf = jnp.maximum(n, max_exact).astype(F32)
    large = max_exact + (jnp.log(nf / max_exact) / math.log(MAX_DISTANCE / max_exact)
                         * (N_BUCKETS - max_exact)).astype(jnp.int32)
    large = jnp.minimum(large, N_BUCKETS - 1)
    return jnp.where(n < max_exact, n, large)


def _bias_rows(rel_bias):
    m = jnp.arange(2 * SUPER)
    delta = jnp.arange(N_NEAR)[:, None]
    rel = delta * SUPER + jnp.where(m < SUPER, m, m - 2 * SUPER)
    w = jnp.where(rel >= 0, rel_bias[:, _t5_causal_bucket(rel)] * LOG2E, NEG).astype(F32)
    return w[:, :, None, :]


def _far_bias(rel_bias):
    c = rel_bias[:, _t5_causal_bucket(jnp.int32(FAR_DIST * MOBA_BLOCK))] * LOG2E
    hi = c.astype(BF16).astype(F32)
    return hi, c - hi


def _proj_kernel(far_hi_ref, far_lo_ref, x_ref, g_ref, win_ref, cw_ref, wco_ref,
                 qt_ref, k_ref, vt_ref, gc_ref, sga_ref,
                 carry_ref, kmean_ref):
    tm = x_ref.shape[1]
    c, a, d = CONV_WIDTH, ATTN_WIDTH, D_MODEL
    s = pl.program_id(1)
    blocks_per_tile = tm // MOBA_BLOCK
    blk0 = s * blocks_per_tile

    @pl.when(s == 0)
    def _():
        carry_ref[...] = jnp.zeros_like(carry_ref)
        kmean_ref[...] = jnp.zeros_like(kmean_ref)

    h = _rms(x_ref[0], g_ref[...]).astype(BF16)

    def seg(lo, width):
        return jnp.dot(h, win_ref[:, lo:lo + width], preferred_element_type=F32)

    u = seg(2 * c, c) * seg(0, c)
    row = lax.broadcasted_iota(jnp.int32, u.shape, 0)
    prev = carry_ref[...]
    p1, p2 = prev[7:8], prev[6:7]
    u1 = jnp.where(row == 0, p1, pltpu.roll(u, 1, 0))
    u2 = jnp.where(row == 0, p2, jnp.where(row == 1, p1, pltpu.roll(u, 2, 0)))
    carry_ref[...] = u[tm - 8:]
    cw = cw_ref[...]
    conv = cw[2:3] * u + cw[1:2] * u1 + cw[0:1] * u2
    z = (seg(c, c) * conv).astype(BF16)
    yc = jnp.dot(z, wco_ref[...], preferred_element_type=F32)
    gc_ref[0] = jax.nn.sigmoid(seg(3 * c + 3 * a, d)) * yc
    sga_ref[0] = jax.nn.sigmoid(seg(3 * c + 3 * a + d, d))

    q = seg(3 * c, a)
    k = seg(3 * c + a, a)
    v = seg(3 * c + 2 * a, a)

    rows_nb = lax.broadcasted_iota(jnp.int32, kmean_ref.shape, 0)
    km = kmean_ref[...]
    for bi in range(blocks_per_tile):
        ks = jnp.sum(k[bi * MOBA_BLOCK:(bi + 1) * MOBA_BLOCK], axis=0, keepdims=True) * (1.0 / MOBA_BLOCK)
        km = jnp.where(rows_nb == blk0 + bi, ks, km)
    kmean_ref[...] = km

    qt = q.T
    vt = v.T
    n_iota = lax.broadcasted_iota(jnp.int32, (MASK_SLOTS, tm), 0)
    tok_blk = blk0 + lax.broadcasted_iota(jnp.int32, (MASK_SLOTS, tm), 1) // MOBA_BLOCK
    past = n_iota < tok_blk
    far = tok_blk - n_iota >= FAR_DIST
    ones_row = jnp.where(lax.broadcasted_iota(jnp.int32, (V_ROWS - HEAD_DIM, tm), 0) == 0, 1.0, 0.0)
    for hd in range(N_HEADS):
        qth = qt[hd * HEAD_DIM:(hd + 1) * HEAD_DIM]
        gate = jnp.dot(km[:, hd * HEAD_DIM:(hd + 1) * HEAD_DIM], qth,
                       precision=lax.Precision.HIGHEST, preferred_element_type=F32)
        g = jnp.where(past, gate, NEG)
        sel = n_iota == tok_blk
        for _ in range(MOBA_TOPK):
            m = jnp.max(g, axis=0, keepdims=True)
            idx = jnp.min(jnp.where(g == m, n_iota, MASK_SLOTS), axis=0, keepdims=True)
            pick = n_iota == idx
            sel = sel | (pick & past)
            g = jnp.where(pick, -jnp.inf, g)
        mask_hi = jnp.where(sel, jnp.where(far, far_hi_ref[hd], 0.0), NEG)
        mask_lo = jnp.where(sel & far, far_lo_ref[hd], 0.0)
        qt_ref[0, hd] = jnp.concatenate(
            [qth * (HEAD_DIM ** -0.5 * LOG2E), mask_hi, mask_lo], axis=0).astype(BF16)
        vt_ref[0, hd] = jnp.concatenate(
            [vt[hd * HEAD_DIM:(hd + 1) * HEAD_DIM], ones_row], axis=0).astype(BF16)

    lane = lax.broadcasted_iota(jnp.int32, (tm, LANES), 1)
    row_blk = blk0 + lax.broadcasted_iota(jnp.int32, (tm, LANES), 0) // MOBA_BLOCK
    k_pat = jnp.where(lane % MASK_SLOTS == row_blk, 1.0, 0.0)
    low = lane < HEAD_DIM
    for hp in range(N_HEADS // 2):
        kt = k[:, hp * LANES:(hp + 1) * LANES]
        k_ref[0, 2 * hp] = jnp.where(low, kt, k_pat).astype(BF16)
        k_ref[0, 2 * hp + 1] = jnp.where(low, pltpu.roll(kt, HEAD_DIM, 1), k_pat).astype(BF16)


def _projection(x, g, w_in, conv_w, w_conv_out, far_hi, far_lo):
    b, s, d = x.shape
    tm = TOKEN_TILE
    const = lambda shape: pl.BlockSpec(shape, lambda bi, si: (0,) * len(shape),
                                       pipeline_mode=pl.Buffered(1))
    smem = pl.BlockSpec(memory_space=pltpu.SMEM)
    tok = pl.BlockSpec((1, tm, d), lambda bi, si: (bi, si, 0))
    return pl.pallas_call(
        _proj_kernel,
        grid=(b, s // tm),
        in_specs=[smem, smem, tok, const((1, d)), const((d, N_IN)), const((CONV_K, CONV_WIDTH)),
                  const((CONV_WIDTH, d))],
        out_specs=[pl.BlockSpec((1, N_HEADS, AUG, tm), lambda bi, si: (bi, 0, 0, si)),
                   pl.BlockSpec((1, N_HEADS, tm, AUG), lambda bi, si: (bi, 0, si, 0)),
                   pl.BlockSpec((1, N_HEADS, V_ROWS, tm), lambda bi, si: (bi, 0, 0, si)),
                   tok, tok],
        out_shape=[jax.ShapeDtypeStruct((b, N_HEADS, AUG, s), BF16),
                   jax.ShapeDtypeStruct((b, N_HEADS, s, AUG), BF16),
                   jax.ShapeDtypeStruct((b, N_HEADS, V_ROWS, s), BF16),
                   jax.ShapeDtypeStruct((b, s, d), F32),
                   jax.ShapeDtypeStruct((b, s, d), F32)],
        scratch_shapes=[pltpu.VMEM((8, CONV_WIDTH), F32),
                        pltpu.VMEM((MASK_SLOTS, ATTN_WIDTH), F32)],
        compiler_params=pltpu.CompilerParams(
            dimension_semantics=("arbitrary", "arbitrary"), vmem_limit_bytes=VMEM_LIMIT),
        name="moba_projection",
    )(far_hi, far_lo, x, g, w_in, conv_w, w_conv_out)


def _attn_kernel(qa_ref, qb_ref, k_ref, vt_ref, brow_ref, oa_ref, ob_ref,
                 q_ref, s_ref, tmax_ref, m_ref, acc_ref, bias_ref, *, n_tiles):
    pr = pl.program_id(2)
    n_items = n_tiles + 1
    n_far = n_items - 2 * N_NEAR
    n_heads = qa_ref.shape[1]
    tile_a, tile_b = pr, n_tiles - 1 - pr

    @pl.when((pl.program_id(1) == 0) & (pr == 0))
    def _():
        kb = lax.broadcasted_iota(jnp.int32, (SUPER, SUPER), 0) // MOBA_BLOCK
        qb = lax.broadcasted_iota(jnp.int32, (SUPER, SUPER), 1) // MOBA_BLOCK
        for hh in range(n_heads):
            for delta in range(N_NEAR):
                rows = jnp.broadcast_to(brow_ref[hh, delta], (SUPER, 2 * SUPER))
                tile = pltpu.roll(rows, 0, 1, stride=1, stride_axis=0)[:, :SUPER]
                dist = (SUPER // MOBA_BLOCK) * delta + qb - kb
                bias_ref[hh, delta] = jnp.where(dist >= FAR_DIST, 0.0, tile)
            bias_ref[hh, N_NEAR] = jnp.zeros((SUPER, SUPER), F32)

    def item(w):
        far_b = jnp.minimum(tile_b + 1 - N_NEAR, n_far)
        rest_b = tile_b + 1 - far_b
        u = w - n_far
        is_far = w < n_far
        second = jnp.where(is_far, w < far_b, u < rest_b)
        t = jnp.where(second, jnp.where(is_far, w, far_b + u),
                      jnp.where(is_far, w - far_b, n_far - far_b + u - rest_b))
        return second.astype(jnp.int32), jnp.where(second, tile_b, tile_a), t

    def produce(w, slot, near, heads=None):
        second, qi, t = item(w)
        keys = pl.ds(pl.multiple_of(t * SUPER, SUPER), SUPER)
        for hh in (range(n_heads) if heads is None else heads):
            st = jnp.dot(k_ref[0, hh, keys, :], q_ref[second, hh], preferred_element_type=F32)
            if near:
                st = st + bias_ref[hh, jnp.minimum(qi - t, N_NEAR)]
            s_ref[slot, hh] = st
            tmax_ref[slot, hh] = jnp.max(st, axis=0, keepdims=True)

    def consume(w, slot, heads=None):
        second, _, t = item(w)
        keys = pl.ds(pl.multiple_of(t * SUPER, SUPER), SUPER)
        for hh in (range(n_heads) if heads is None else heads):
            m = m_ref[second, hh]
            m_new = jnp.maximum(m, tmax_ref[slot, hh])
            p = jnp.exp2(s_ref[slot, hh] - m_new).astype(BF16)
            pv = jnp.dot(vt_ref[0, hh, :, keys], p, preferred_element_type=F32)
            acc_ref[second, hh] = jnp.exp2(m - m_new) * acc_ref[second, hh] + pv
            m_ref[second, hh] = m_new

    q_ref[0] = qa_ref[0]
    q_ref[1] = qb_ref[0]
    m_ref[...] = jnp.full(m_ref.shape, -jnp.inf, F32)
    acc_ref[...] = jnp.zeros(acc_ref.shape, F32)
    produce(0, 0, near=False)

    def pair(jj, _, near):
        for hh in range(n_heads):
            produce(2 * jj + 1, 1, near, [hh])
            consume(2 * jj, 0, [hh])
        for hh in range(n_heads):
            produce(2 * jj + 2, 0, near, [hh])
            consume(2 * jj + 1, 1, [hh])
        return _

    far_pairs = (n_far - 1) // 2
    lax.fori_loop(0, far_pairs, functools.partial(pair, near=False), 0, unroll=2)
    lax.fori_loop(far_pairs, n_items // 2, functools.partial(pair, near=True), 0, unroll=2)
    consume(n_items - 1, 0)

    for second, o_ref in enumerate((oa_ref, ob_ref)):
        outs = [(acc_ref[second, hh, :HEAD_DIM] / acc_ref[second, hh, HEAD_DIM:HEAD_DIM + 1]).T
                for hh in range(n_heads)]
        o_ref[0] = jnp.concatenate(outs, axis=-1).astype(o_ref.dtype)


def _attention(qt, k_aug, vt, bias_rows):
    b, _, _, s = qt.shape
    hb = 4
    n_tiles = s // SUPER
    half = n_tiles // 2
    q_spec = lambda tile: pl.BlockSpec((1, hb, AUG, SUPER), lambda hp, bi, pr: (bi, hp, 0, tile(pr)))
    o_spec = lambda tile: pl.BlockSpec((1, SUPER, hb * HEAD_DIM), lambda hp, bi, pr: (bi, tile(pr), hp))
    o_shape = jax.ShapeDtypeStruct((b, s // 2, ATTN_WIDTH), BF16)
    return pl.pallas_call(
        functools.partial(_attn_kernel, n_tiles=n_tiles),
        grid=(N_HEADS // hb, b, half),
        in_specs=[q_spec(lambda pr: pr), q_spec(lambda pr: n_tiles - 1 - pr),
                  pl.BlockSpec((1, hb, s, AUG), lambda hp, bi, pr: (bi, hp, 0, 0),
                               pipeline_mode=pl.Buffered(1)),
                  pl.BlockSpec((1, hb, V_ROWS, s), lambda hp, bi, pr: (bi, hp, 0, 0),
                               pipeline_mode=pl.Buffered(1)),
                  pl.BlockSpec((hb, N_NEAR, 1, 2 * SUPER), lambda hp, bi, pr: (hp, 0, 0, 0))],
        out_specs=[o_spec(lambda pr: pr), o_spec(lambda pr: half - 1 - pr)],
        out_shape=[o_shape, o_shape],
        scratch_shapes=[pltpu.VMEM((2, hb, AUG, SUPER), BF16),
                        pltpu.VMEM((2, hb, SUPER, SUPER), F32),
                        pltpu.VMEM((2, hb, 1, SUPER), F32),
                        pltpu.VMEM((2, hb, 1, SUPER), F32),
                        pltpu.VMEM((2, hb, V_ROWS, SUPER), F32),
                        pltpu.VMEM((hb, N_NEAR + 1, SUPER, SUPER), F32)],
        compiler_params=pltpu.CompilerParams(
            dimension_semantics=("arbitrary", "arbitrary", "arbitrary"),
            vmem_limit_bytes=VMEM_LIMIT),
        name="moba_attention",
    )(qt, qt, k_aug, vt, bias_rows)


def _out_kernel(x_ref, att_lo_ref, att_hi_ref, gc_ref, sga_ref, wao_ref, wo_ref, gpost_ref, gpre_ref,
                w1_ref, w2_ref, gmpost_ref, o_ref):
    first_half = pl.program_id(1) < pl.num_programs(1) // 2
    att = jnp.where(first_half, att_lo_ref[0], att_hi_ref[0])
    ya = jnp.dot(att, wao_ref[...], preferred_element_type=F32)
    m = (gc_ref[0] + sga_ref[0] * ya).astype(BF16)
    mix = jnp.dot(m, wo_ref[...], preferred_element_type=F32)
    x1 = x_ref[0] + _rms(mix, gpost_ref[...])
    h2 = _rms(x1, gpre_ref[...]).astype(BF16)
    f = jnp.zeros(x1.shape, F32)
    for ci in range(D_FF // FF_CHUNK):
        cols = slice(ci * FF_CHUNK, (ci + 1) * FF_CHUNK)
        act = jnp.dot(h2, w1_ref[:, cols], preferred_element_type=F32)
        act = jnp.square(jnp.maximum(act, 0.0)).astype(BF16)
        f = f + jnp.dot(act, w2_ref[cols, :], preferred_element_type=F32)
    o_ref[0] = x1 + _rms(f, gmpost_ref[...])


def _output(x, att_lo, att_hi, gc, sga, w_attn_out, w_o, g_post, g_mlp_pre, w1, w2, g_mlp_post):
    b, s, d = x.shape
    tm = TOKEN_TILE
    half = s // tm // 2
    const = lambda shape: pl.BlockSpec(shape, lambda bi, si: (0,) * len(shape),
                                       pipeline_mode=pl.Buffered(1))
    tok = pl.BlockSpec((1, tm, d), lambda bi, si: (bi, si, 0))
    return pl.pallas_call(
        _out_kernel,
        grid=(b, s // tm),
        in_specs=[tok,
                  pl.BlockSpec((1, tm, ATTN_WIDTH), lambda bi, si: (bi, jnp.minimum(si, half - 1), 0)),
                  pl.BlockSpec((1, tm, ATTN_WIDTH), lambda bi, si: (bi, jnp.maximum(si - half, 0), 0)),
                  tok, tok,
                  const((ATTN_WIDTH, d)), const((d, d)), const((1, d)), const((1, d)),
                  const((d, D_FF)), const((D_FF, d)), const((1, d))],
        out_specs=tok,
        out_shape=jax.ShapeDtypeStruct((b, s, d), F32),
        compiler_params=pltpu.CompilerParams(
            dimension_semantics=("arbitrary", "arbitrary"), vmem_limit_bytes=VMEM_LIMIT),
        name="moba_output_mlp",
    )(x, att_lo, att_hi, gc, sga, w_attn_out, w_o, g_post, g_mlp_pre, w1, w2, g_mlp_post)


def kernel(x, ln_mix_pre, w_in, conv_w, w_conv_out, w_attn_out, rel_bias, w_o, ln_mix_post,
           ln_mlp_pre, w_mlp_in, w_mlp_out, ln_mlp_post):
    depth = w_in.shape[0]
    b, s, d = x.shape
    assert d == D_MODEL and TOKEN_TILE == SUPER and s % (2 * SUPER) == 0 and s // SUPER >= 2 * N_NEAR
    assert MOBA_TOPK <= s // MOBA_BLOCK <= MASK_SLOTS
    bias_rows = _bias_rows(rel_bias)
    far_hi, far_lo = _far_bias(rel_bias)
    for l in range(depth):
        qt, k_aug, vt, gc, sga = _projection(
            x, ln_mix_pre[l][None], w_in[l].astype(BF16), conv_w[l], w_conv_out[l].astype(BF16),
            far_hi, far_lo)
        att_lo, att_hi = _attention(qt, k_aug, vt, bias_rows)
        x = _output(x, att_lo, att_hi, gc, sga, w_attn_out[l].astype(BF16), w_o[l].astype(BF16),
                    ln_mix_post[l][None], ln_mlp_pre[l][None], w_mlp_in[l].astype(BF16),
                    w_mlp_out[l].astype(BF16), ln_mlp_post[l][None])
    return x
```

```python
import functools
import math

import jax
import jax.numpy as jnp
from jax import lax
from jax.experimental import pallas as pl
from jax.experimental.pallas import tpu as pltpu

D_MODEL = 1024
CONV_WIDTH = D_MODEL // 2
CONV_K = 3
N_HEADS = 8
HEAD_DIM = 64
ATTN_WIDTH = N_HEADS * HEAD_DIM
MOBA_BLOCK = 256
MOBA_TOPK = 3
N_BUCKETS = 32
MAX_DISTANCE = 1024
D_FF = 4 * D_MODEL
RMS_EPS = 1e-6
N_IN = 3 * CONV_WIDTH + 3 * ATTN_WIDTH + 2 * D_MODEL
NEG = -1e30
LOG2E = math.log2(math.e)

LANES = 128
BF16_SUBLANES = 16
AUG = 2 * HEAD_DIM
MASK_SLOTS = (AUG - HEAD_DIM) // 2
V_ROWS = HEAD_DIM + BF16_SUBLANES
SUPER = 2 * MOBA_BLOCK
FAR_DIST = -(-(MAX_DISTANCE + MOBA_BLOCK) // MOBA_BLOCK)
N_NEAR = (FAR_DIST + 1) // 2
TOKEN_TILE = 512
FF_CHUNK = 1024
VMEM_LIMIT = 56 * 1024 * 1024

F32 = jnp.float32
BF16 = jnp.bfloat16


def _rms(x, g):
    return (x * lax.rsqrt(jnp.mean(x * x, axis=-1, keepdims=True) + RMS_EPS)) * g


def _t5_causal_bucket(rel):
    n = jnp.maximum(rel, 0)
    max_exact = N_BUCKETS // 2
    nf = jnp.maximum(n, max_exact).astype(F32)
    large = max_exact + (jnp.log(nf / max_exact) / math.log(MAX_DISTANCE / max_exact)
                         * (N_BUCKETS - max_exact)).astype(jnp.int32)
    large = jnp.minimum(large, N_BUCKETS - 1)
    return jnp.where(n < max_exact, n, large)


def _bias_rows(rel_bias):
    m = jnp.arange(2 * SUPER)
    delta = jnp.arange(N_NEAR)[:, None]
    rel = delta * SUPER + jnp.where(m < SUPER, m, m - 2 * SUPER)
    w = jnp.where(rel >= 0, rel_bias[:, _t5_causal_bucket(rel)] * LOG2E, NEG).astype(F32)
    return w[:, :, None, :]


def _far_bias(rel_bias):
    c = rel_bias[:, _t5_causal_bucket(jnp.int32(FAR_DIST * MOBA_BLOCK))] * LOG2E
    hi = c.astype(BF16).astype(F32)
    return hi, c - hi


def _proj_kernel(far_hi_ref, far_lo_ref, x_ref, g_ref, win_ref, cw_ref, wco_ref,
                 qt_ref, k_ref, vt_ref, gc_ref, sga_ref,
                 carry_ref, kmean_ref):
    tm = x_ref.shape[1]
    c, a, d = CONV_WIDTH, ATTN_WIDTH, D_MODEL
    s = pl.program_id(1)
    blocks_per_tile = tm // MOBA_BLOCK
    blk0 = s * blocks_per_tile

    @pl.when(s == 0)
    def _():
        carry_ref[...] = jnp.zeros_like(carry_ref)
        kmean_ref[...] = jnp.zeros_like(kmean_ref)

    h = _rms(x_ref[0], g_ref[...]).astype(BF16)

    def seg(lo, width):
        return jnp.dot(h, win_ref[:, lo:lo + width], preferred_element_type=F32)

    q = seg(3 * c, a)
    k = seg(3 * c + a, a)

    rows_nb = lax.broadcasted_iota(jnp.int32, kmean_ref.shape, 0)
    km = kmean_ref[...]
    for bi in range(blocks_per_tile):
        ks = jnp.sum(k[bi * MOBA_BLOCK:(bi + 1) * MOBA_BLOCK], axis=0, keepdims=True) * (1.0 / MOBA_BLOCK)
        km = jnp.where(rows_nb == blk0 + bi, ks, km)
    kmean_ref[...] = km

    qt = q.T
    n_iota = lax.broadcasted_iota(jnp.int32, (MASK_SLOTS, tm), 0)
    tok_blk = blk0 + lax.broadcasted_iota(jnp.int32, (MASK_SLOTS, tm), 1) // MOBA_BLOCK
    past = n_iota < tok_blk
    far = tok_blk - n_iota >= FAR_DIST
    dh = d // 2
    later = [(3 * c + 2 * a, a), (0, c), (2 * c, c), (c, c),
             (3 * c + 3 * a, dh), (3 * c + 3 * a + dh, dh),
             (3 * c + 3 * a + d, dh), (3 * c + 3 * a + d + dh, dh)]
    assert len(later) == N_HEADS
    segs = []
    for hd in range(N_HEADS):
        qth = qt[hd * HEAD_DIM:(hd + 1) * HEAD_DIM]
        gate = jnp.dot(km[:, hd * HEAD_DIM:(hd + 1) * HEAD_DIM].astype(BF16), qth.astype(BF16),
                       preferred_element_type=F32)
        g = jnp.where(past, gate, NEG)
        sel = n_iota == tok_blk
        for _ in range(MOBA_TOPK):
            m = jnp.max(g, axis=0, keepdims=True)
            idx = jnp.min(jnp.where(g == m, n_iota, MASK_SLOTS), axis=0, keepdims=True)
            pick = n_iota == idx
            sel = sel | (pick & past)
            g = jnp.where(pick, -jnp.inf, g)
        mask_hi = jnp.where(sel, jnp.where(far, far_hi_ref[hd], 0.0), NEG)
        mask_lo = jnp.where(sel & far, far_lo_ref[hd], 0.0)
        qt_ref[0, hd] = jnp.concatenate(
            [qth * (HEAD_DIM ** -0.5 * LOG2E), mask_hi, mask_lo], axis=0).astype(BF16)
        segs.append(seg(*later[hd]))
    v, xin, gcv, gb, gconv0, gconv1, gattn0, gattn1 = segs

    vt = v.T
    ones_row = jnp.where(lax.broadcasted_iota(jnp.int32, (V_ROWS - HEAD_DIM, tm), 0) == 0, 1.0, 0.0)
    for hd in range(N_HEADS):
        vt_ref[0, hd] = jnp.concatenate(
            [vt[hd * HEAD_DIM:(hd + 1) * HEAD_DIM], ones_row], axis=0).astype(BF16)

    lane = lax.broadcasted_iota(jnp.int32, (tm, LANES), 1)
    row_blk = blk0 + lax.broadcasted_iota(jnp.int32, (tm, LANES), 0) // MOBA_BLOCK
    k_pat = jnp.where(lane % MASK_SLOTS == row_blk, 1.0, 0.0)
    low = lane < HEAD_DIM
    for hp in range(N_HEADS // 2):
        kt = k[:, hp * LANES:(hp + 1) * LANES]
        k_ref[0, 2 * hp] = jnp.where(low, kt, k_pat).astype(BF16)
        k_ref[0, 2 * hp + 1] = jnp.where(low, pltpu.roll(kt, HEAD_DIM, 1), k_pat).astype(BF16)

    u = gcv * xin
    row = lax.broadcasted_iota(jnp.int32, u.shape, 0)
    prev = carry_ref[...]
    p1, p2 = prev[7:8], prev[6:7]
    u1 = jnp.where(row == 0, p1, pltpu.roll(u, 1, 0))
    u2 = jnp.where(row == 0, p2, jnp.where(row == 1, p1, pltpu.roll(u, 2, 0)))
    carry_ref[...] = u[tm - 8:]
    cw = cw_ref[...]
    conv = cw[2:3] * u + cw[1:2] * u1 + cw[0:1] * u2
    z = (gb * conv).astype(BF16)
    yc = jnp.dot(z, wco_ref[...], preferred_element_type=F32)
    gc_ref[0, :, :dh] = jax.nn.sigmoid(gconv0) * yc[:, :dh]
    gc_ref[0, :, dh:] = jax.nn.sigmoid(gconv1) * yc[:, dh:]
    sga_ref[0, :, :dh] = jax.nn.sigmoid(gattn0)
    sga_ref[0, :, dh:] = jax.nn.sigmoid(gattn1)


def _projection(x, g, w_in, conv_w, w_conv_out, far_hi, far_lo):
    b, s, d = x.shape
    tm = TOKEN_TILE
    const = lambda shape: pl.BlockSpec(shape, lambda bi, si: (0,) * len(shape),
                                       pipeline_mode=pl.Buffered(1))
    smem = pl.BlockSpec(memory_space=pltpu.SMEM)
    tok = pl.BlockSpec((1, tm, d), lambda bi, si: (bi, si, 0))
    return pl.pallas_call(
        _proj_kernel,
        grid=(b, s // tm),
        in_specs=[smem, smem, tok, const((1, d)), const((d, N_IN)), const((CONV_K, CONV_WIDTH)),
                  const((CONV_WIDTH, d))],
        out_specs=[pl.BlockSpec((1, N_HEADS, AUG, tm), lambda bi, si: (bi, 0, 0, si)),
                   pl.BlockSpec((1, N_HEADS, tm, AUG), lambda bi, si: (bi, 0, si, 0)),
                   pl.BlockSpec((1, N_HEADS, V_ROWS, tm), lambda bi, si: (bi, 0, 0, si)),
                   tok, tok],
        out_shape=[jax.ShapeDtypeStruct((b, N_HEADS, AUG, s), BF16),
                   jax.ShapeDtypeStruct((b, N_HEADS, s, AUG), BF16),
                   jax.ShapeDtypeStruct((b, N_HEADS, V_ROWS, s), BF16),
                   jax.ShapeDtypeStruct((b, s, d), F32),
                   jax.ShapeDtypeStruct((b, s, d), F32)],
        scratch_shapes=[pltpu.VMEM((8, CONV_WIDTH), F32),
                        pltpu.VMEM((MASK_SLOTS, ATTN_WIDTH), F32)],
        compiler_params=pltpu.CompilerParams(
            dimension_semantics=("arbitrary", "arbitrary"), vmem_limit_bytes=VMEM_LIMIT),
        name="moba_projection",
    )(far_hi, far_lo, x, g, w_in, conv_w, w_conv_out)


def _attn_kernel(qa_ref, qb_ref, k_ref, vt_ref, brow_ref, oa_ref, ob_ref,
                 q_ref, s_ref, tmax_ref, m_ref, acc_ref, bias_ref, *, n_tiles):
    pr = pl.program_id(2)
    n_items = n_tiles + 1
    n_far = n_items - 2 * N_NEAR
    n_heads = qa_ref.shape[1]
    tile_a, tile_b = pr, n_tiles - 1 - pr

    @pl.when((pl.program_id(1) == 0) & (pr == 0))
    def _():
        kb = lax.broadcasted_iota(jnp.int32, (SUPER, SUPER), 0) // MOBA_BLOCK
        qb = lax.broadcasted_iota(jnp.int32, (SUPER, SUPER), 1) // MOBA_BLOCK
        for hh in range(n_heads):
            for delta in range(N_NEAR):
                rows = jnp.broadcast_to(brow_ref[hh, delta], (SUPER, 2 * SUPER))
                tile = pltpu.roll(rows, 0, 1, stride=1, stride_axis=0)[:, :SUPER]
                dist = (SUPER // MOBA_BLOCK) * delta + qb - kb
                bias_ref[hh, delta] = jnp.where(dist >= FAR_DIST, 0.0, tile)
            bias_ref[hh, N_NEAR] = jnp.zeros((SUPER, SUPER), F32)

    def item(w):
        far_b = jnp.minimum(tile_b + 1 - N_NEAR, n_far)
        rest_b = tile_b + 1 - far_b
        u = w - n_far
        is_far = w < n_far
        second = jnp.where(is_far, w < far_b, u < rest_b)
        t = jnp.where(second, jnp.where(is_far, w, far_b + u),
                      jnp.where(is_far, w - far_b, n_far - far_b + u - rest_b))
        return second.astype(jnp.int32), jnp.where(second, tile_b, tile_a), t

    def produce(w, slot, near, heads=None):
        second, qi, t = item(w)
        keys = pl.ds(pl.multiple_of(t * SUPER, SUPER), SUPER)
        for hh in (range(n_heads) if heads is None else heads):
            st = jnp.dot(k_ref[0, hh, keys, :], q_ref[second, hh], preferred_element_type=F32)
            if near:
                st = st + bias_ref[hh, jnp.minimum(qi - t, N_NEAR)]
            s_ref[slot, hh] = st
            tmax_ref[slot, hh] = jnp.max(st, axis=0, keepdims=True)

    def consume(w, slot, heads=None):
        second, _, t = item(w)
        keys = pl.ds(pl.multiple_of(t * SUPER, SUPER), SUPER)
        for hh in (range(n_heads) if heads is None else heads):
            m = m_ref[second, hh]
            m_new = jnp.maximum(m, tmax_ref[slot, hh])
            p = jnp.exp2(s_ref[slot, hh] - m_new).astype(BF16)
            pv = jnp.dot(vt_ref[0, hh, :, keys], p, preferred_element_type=F32)
            acc_ref[second, hh] = jnp.exp2(m - m_new) * acc_ref[second, hh] + pv
            m_ref[second, hh] = m_new

    q_ref[0] = qa_ref[0]
    q_ref[1] = qb_ref[0]
    m_ref[...] = jnp.full(m_ref.shape, -jnp.inf, F32)
    acc_ref[...] = jnp.zeros(acc_ref.shape, F32)
    produce(0, 0, near=False)

    def pair(jj, _, near):
        for hh in range(n_heads):
            produce(2 * jj + 1, 1, near, [hh])
            consume(2 * jj, 0, [hh])
        for hh in range(n_heads):
            produce(2 * jj + 2, 0, near, [hh])
            consume(2 * jj + 1, 1, [hh])
        return _

    far_pairs = (n_far - 1) // 2
    lax.fori_loop(0, far_pairs, functools.partial(pair, near=False), 0, unroll=2)
    lax.fori_loop(far_pairs, n_items // 2, functools.partial(pair, near=True), 0, unroll=2)
    consume(n_items - 1, 0)

    for second, o_ref in enumerate((oa_ref, ob_ref)):
        outs = [(acc_ref[second, hh, :HEAD_DIM] / acc_ref[second, hh, HEAD_DIM:HEAD_DIM + 1]).T
                for hh in range(n_heads)]
        o_ref[0] = jnp.concatenate(outs, axis=-1).astype(o_ref.dtype)


def _attention(qt, k_aug, vt, bias_rows):
    b, _, _, s = qt.shape
    hb = 4
    n_tiles = s // SUPER
    half = n_tiles // 2
    q_spec = lambda tile: pl.BlockSpec((1, hb, AUG, SUPER), lambda hp, bi, pr: (bi, hp, 0, tile(pr)))
    o_spec = lambda tile: pl.BlockSpec((1, SUPER, hb * HEAD_DIM), lambda hp, bi, pr: (bi, tile(pr), hp))
    o_shape = jax.ShapeDtypeStruct((b, s // 2, ATTN_WIDTH), BF16)
    return pl.pallas_call(
        functools.partial(_attn_kernel, n_tiles=n_tiles),
        grid=(N_HEADS // hb, b, half),
        in_specs=[q_spec(lambda pr: pr), q_spec(lambda pr: n_tiles - 1 - pr),
                  pl.BlockSpec((1, hb, s, AUG), lambda hp, bi, pr: (bi, hp, 0, 0),
                               pipeline_mode=pl.Buffered(1)),
                  pl.BlockSpec((1, hb, V_ROWS, s), lambda hp, bi, pr: (bi, hp, 0, 0),
                               pipeline_mode=pl.Buffered(1)),
                  pl.BlockSpec((hb, N_NEAR, 1, 2 * SUPER), lambda hp, bi, pr: (hp, 0, 0, 0))],
        out_specs=[o_spec(lambda pr: pr), o_spec(lambda pr: half - 1 - pr)],
        out_shape=[o_shape, o_shape],
        scratch_shapes=[pltpu.VMEM((2, hb, AUG, SUPER), BF16),
                        pltpu.VMEM((2, hb, SUPER, SUPER), F32),
                        pltpu.VMEM((2, hb, 1, SUPER), F32),
                        pltpu.VMEM((2, hb, 1, SUPER), F32),
                        pltpu.VMEM((2, hb, V_ROWS, SUPER), F32),
                        pltpu.VMEM((hb, N_NEAR + 1, SUPER, SUPER), F32)],
        compiler_params=pltpu.CompilerParams(
            dimension_semantics=("arbitrary", "arbitrary", "arbitrary"),
            vmem_limit_bytes=VMEM_LIMIT),
        name="moba_attention",
    )(qt, qt, k_aug, vt, bias_rows)


def _out_kernel(x_ref, att_lo_ref, att_hi_ref, gc_ref, sga_ref, wao_ref, wo_ref, gpost_ref, gpre_ref,
                w1_ref, w2_ref, gmpost_ref, o_ref):
    first_half = pl.program_id(1) < pl.num_programs(1) // 2
    att = jnp.where(first_half, att_lo_ref[0], att_hi_ref[0])
    ya = jnp.dot(att, wao_ref[...], preferred_element_type=F32)
    m = (gc_ref[0] + sga_ref[0] * ya).astype(BF16)
    mix = jnp.dot(m, wo_ref[...], preferred_element_type=F32)
    x1 = x_ref[0] + _rms(mix, gpost_ref[...])
    h2 = _rms(x1, gpre_ref[...]).astype(BF16)
    f = jnp.zeros(x1.shape, F32)
    for ci in range(D_FF // FF_CHUNK):
        cols = slice(ci * FF_CHUNK, (ci + 1) * FF_CHUNK)
        act = jnp.dot(h2, w1_ref[:, cols], preferred_element_type=F32)
        act = jnp.square(jnp.maximum(act, 0.0)).astype(BF16)
        f = f + jnp.dot(act, w2_ref[cols, :], preferred_element_type=F32)
    o_ref[0] = x1 + _rms(f, gmpost_ref[...])


def _output(x, att_lo, att_hi, gc, sga, w_attn_out, w_o, g_post, g_mlp_pre, w1, w2, g_mlp_post):
    b, s, d = x.shape
    tm = TOKEN_TILE
    half = s // tm // 2
    const = lambda shape: pl.BlockSpec(shape, lambda bi, si: (0,) * len(shape),
                                       pipeline_mode=pl.Buffered(1))
    tok = pl.BlockSpec((1, tm, d), lambda bi, si: (bi, si, 0))
    return pl.pallas_call(
        _out_kernel,
        grid=(b, s // tm),
        in_specs=[tok,
                  pl.BlockSpec((1, tm, ATTN_WIDTH), lambda bi, si: (bi, jnp.minimum(si, half - 1), 0)),
                  pl.BlockSpec((1, tm, ATTN_WIDTH), lambda bi, si: (bi, jnp.maximum(si - half, 0), 0)),
                  tok, tok,
                  const((ATTN_WIDTH, d)), const((d, d)), const((1, d)), const((1, d)),
                  const((d, D_FF)), const((D_FF, d)), const((1, d))],
        out_specs=tok,
        out_shape=jax.ShapeDtypeStruct((b, s, d), F32),
        compiler_params=pltpu.CompilerParams(
            dimension_semantics=("arbitrary", "arbitrary"), vmem_limit_bytes=VMEM_LIMIT),
        name="moba_output_mlp",
    )(x, att_lo, att_hi, gc, sga, w_attn_out, w_o, g_post, g_mlp_pre, w1, w2, g_mlp_post)


def kernel(x, ln_mix_pre, w_in, conv_w, w_conv_out, w_attn_out, rel_bias, w_o, ln_mix_post,
           ln_mlp_pre, w_mlp_in, w_mlp_out, ln_mlp_post):
    depth = w_in.shape[0]
    b, s, d = x.shape
    assert d == D_MODEL and TOKEN_TILE == SUPER and s % (2 * SUPER) == 0 and s // SUPER >= 2 * N_NEAR
    assert MOBA_TOPK <= s // MOBA_BLOCK <= MASK_SLOTS
    bias_rows = _bias_rows(rel_bias)
    far_hi, far_lo = _far_bias(rel_bias)
    for l in range(depth):
        qt, k_aug, vt, gc, sga = _projection(
            x, ln_mix_pre[l][None], w_in[l].astype(BF16), conv_w[l], w_conv_out[l].astype(BF16),
            far_hi, far_lo)
        att_lo, att_hi = _attention(qt, k_aug, vt, bias_rows)
        x = _output(x, att_lo, att_hi, gc, sga, w_attn_out[l].astype(BF16), w_o[l].astype(BF16),
                    ln_mix_post[l][None], ln_mlp_pre[l][None], w_mlp_in[l].astype(BF16),
                    w_mlp_out[l].astype(BF16), ln_mlp_post[l][None])
    return x
```

```python
import functools
import math

import jax
import jax.numpy as jnp
from jax import lax
from jax.experimental import pallas as pl
from jax.experimental.pallas import tpu as pltpu

D_MODEL = 1024
CONV_WIDTH = D_MODEL // 2
CONV_K = 3
N_HEADS = 8
HEAD_DIM = 64
ATTN_WIDTH = N_HEADS * HEAD_DIM
MOBA_BLOCK = 256
MOBA_TOPK = 3
N_BUCKETS = 32
MAX_DISTANCE = 1024
D_FF = 4 * D_MODEL
RMS_EPS = 1e-6
N_IN = 3 * CONV_WIDTH + 3 * ATTN_WIDTH + 2 * D_MODEL
NEG = -1e30
LOG2E = math.log2(math.e)

LANES = 128
BF16_SUBLANES = 16
AUG = 2 * HEAD_DIM
MASK_SLOTS = (AUG - HEAD_DIM) // 2
V_ROWS = HEAD_DIM + BF16_SUBLANES
SUPER = 2 * MOBA_BLOCK
FAR_DIST = -(-(MAX_DISTANCE + MOBA_BLOCK) // MOBA_BLOCK)
N_NEAR = (FAR_DIST + 1) // 2
TOKEN_TILE = 512
FF_CHUNK = 1024
VMEM_LIMIT = 56 * 1024 * 1024

F32 = jnp.float32
BF16 = jnp.bfloat16


def _rms(x, g):
    return (x * lax.rsqrt(jnp.mean(x * x, axis=-1, keepdims=True) + RMS_EPS)) * g


def _t5_causal_bucket(rel):
    n = jnp.maximum(rel, 0)
    max_exact = N_BUCKETS // 2
    nf = jnp.maximum(n, max_exact).astype(F32)
    large = max_exact + (jnp.log(nf / max_exact) / math.log(MAX_DISTANCE / max_exact)
                         * (N_BUCKETS - max_exact)).astype(jnp.int32)
    large = jnp.minimum(large, N_BUCKETS - 1)
    return jnp.where(n < max_exact, n, large)


def _bias_rows(rel_bias):
    m = jnp.arange(2 * SUPER)
    delta = jnp.arange(N_NEAR)[:, None]
    rel = delta * SUPER + jnp.where(m < SUPER, m, m - 2 * SUPER)
    w = jnp.where(rel >= 0, rel_bias[:, _t5_causal_bucket(rel)] * LOG2E, NEG).astype(F32)
    return w[:, :, None, :]


def _far_bias(rel_bias):
    c = rel_bias[:, _t5_causal_bucket(jnp.int32(FAR_DIST * MOBA_BLOCK))] * LOG2E
    hi = c.astype(BF16).astype(F32)
    return hi, c - hi


def _proj_kernel(far_hi_ref, far_lo_ref, x_ref, g_ref, win_ref, cw_ref, wco_ref,
                 qt_ref, k_ref, vt_ref, gc_ref, sga_ref,
                 carry_ref, kmean_ref):
    tm = x_ref.shape[1]
    c, a, d = CONV_WIDTH, ATTN_WIDTH, D_MODEL
    s = pl.program_id(1)
    blocks_per_tile = tm // MOBA_BLOCK
    blk0 = s * blocks_per_tile

    @pl.when(s == 0)
    def _():
        carry_ref[...] = jnp.zeros_like(carry_ref)
        kmean_ref[...] = jnp.zeros_like(kmean_ref)

    h = _rms(x_ref[0], g_ref[...]).astype(BF16)

    def seg(lo, width):
        return jnp.dot(h, win_ref[:, lo:lo + width], preferred_element_type=F32)

    q = seg(3 * c, a)
    k = seg(3 * c + a, a)

    rows_nb = lax.broadcasted_iota(jnp.int32, kmean_ref.shape, 0)
    km = kmean_ref[...]
    for bi in range(blocks_per_tile):
        ks = jnp.sum(k[bi * MOBA_BLOCK:(bi + 1) * MOBA_BLOCK], axis=0, keepdims=True) * (1.0 / MOBA_BLOCK)
        km = jnp.where(rows_nb == blk0 + bi, ks, km)
    kmean_ref[...] = km

    qt = q.T
    n_iota = lax.broadcasted_iota(jnp.int32, (MASK_SLOTS, tm), 0)
    tok_blk = blk0 + lax.broadcasted_iota(jnp.int32, (MASK_SLOTS, tm), 1) // MOBA_BLOCK
    past = n_iota < tok_blk
    far = tok_blk - n_iota >= FAR_DIST
    dh = d // 2
    later = [(3 * c + 2 * a, a), (0, c), (2 * c, c), (c, c),
             (3 * c + 3 * a, dh), (3 * c + 3 * a + dh, dh),
             (3 * c + 3 * a + d, dh), (3 * c + 3 * a + d + dh, dh)]
    assert len(later) == N_HEADS
    segs = []
    for hd in range(N_HEADS):
        qth = qt[hd * HEAD_DIM:(hd + 1) * HEAD_DIM]
        gate = jnp.dot(km[:, hd * HEAD_DIM:(hd + 1) * HEAD_DIM].astype(BF16), qth.astype(BF16),
                       preferred_element_type=F32)
        g = jnp.where(past, gate, NEG)
        sel = n_iota == tok_blk
        for _ in range(MOBA_TOPK):
            m = jnp.max(g, axis=0, keepdims=True)
            idx = jnp.min(jnp.where(g == m, n_iota, MASK_SLOTS), axis=0, keepdims=True)
            pick = n_iota == idx
            sel = sel | (pick & past)
            g = jnp.where(pick, -jnp.inf, g)
        mask_hi = jnp.where(sel, jnp.where(far, far_hi_ref[hd], 0.0), NEG)
        mask_lo = jnp.where(sel & far, far_lo_ref[hd], 0.0)
        qt_ref[0, hd, 0] = jnp.concatenate(
            [qth * (HEAD_DIM ** -0.5 * LOG2E), mask_hi, mask_lo], axis=0).astype(BF16)
        segs.append(seg(*later[hd]))
    v, xin, gcv, gb, gconv0, gconv1, gattn0, gattn1 = segs

    vt = v.T
    ones_row = jnp.where(lax.broadcasted_iota(jnp.int32, (V_ROWS - HEAD_DIM, tm), 0) == 0, 1.0, 0.0)
    for hd in range(N_HEADS):
        vt_ref[0, hd, 0] = jnp.concatenate(
            [vt[hd * HEAD_DIM:(hd + 1) * HEAD_DIM], ones_row], axis=0).astype(BF16)

    lane = lax.broadcasted_iota(jnp.int32, (tm, LANES), 1)
    row_blk = blk0 + lax.broadcasted_iota(jnp.int32, (tm, LANES), 0) // MOBA_BLOCK
    k_pat = jnp.where(lane % MASK_SLOTS == row_blk, 1.0, 0.0)
    low = lane < HEAD_DIM
    for hp in range(N_HEADS // 2):
        kt = k[:, hp * LANES:(hp + 1) * LANES]
        k_ref[0, 2 * hp] = jnp.where(low, kt, k_pat).astype(BF16)
        k_ref[0, 2 * hp + 1] = jnp.where(low, pltpu.roll(kt, HEAD_DIM, 1), k_pat).astype(BF16)

    u = gcv * xin
    row = lax.broadcasted_iota(jnp.int32, u.shape, 0)
    prev = carry_ref[...]
    p1, p2 = prev[7:8], prev[6:7]
    u1 = jnp.where(row == 0, p1, pltpu.roll(u, 1, 0))
    u2 = jnp.where(row == 0, p2, jnp.where(row == 1, p1, pltpu.roll(u, 2, 0)))
    carry_ref[...] = u[tm - 8:]
    cw = cw_ref[...]
    conv = cw[2:3] * u + cw[1:2] * u1 + cw[0:1] * u2
    z = (gb * conv).astype(BF16)
    yc = jnp.dot(z, wco_ref[...], preferred_element_type=F32)
    gc_ref[0, :, :dh] = jax.nn.sigmoid(gconv0) * yc[:, :dh]
    gc_ref[0, :, dh:] = jax.nn.sigmoid(gconv1) * yc[:, dh:]
    sga_ref[0, :, :dh] = jax.nn.sigmoid(gattn0)
    sga_ref[0, :, dh:] = jax.nn.sigmoid(gattn1)


def _projection(x, g, w_in, conv_w, w_conv_out, far_hi, far_lo):
    b, s, d = x.shape
    tm = TOKEN_TILE
    const = lambda shape: pl.BlockSpec(shape, lambda bi, si: (0,) * len(shape),
                                       pipeline_mode=pl.Buffered(1))
    smem = pl.BlockSpec(memory_space=pltpu.SMEM)
    tok = pl.BlockSpec((1, tm, d), lambda bi, si: (bi, si, 0))
    return pl.pallas_call(
        _proj_kernel,
        grid=(b, s // tm),
        in_specs=[smem, smem, tok, const((1, d)), const((d, N_IN)), const((CONV_K, CONV_WIDTH)),
                  const((CONV_WIDTH, d))],
        out_specs=[pl.BlockSpec((1, N_HEADS, 1, AUG, tm), lambda bi, si: (bi, 0, si, 0, 0)),
                   pl.BlockSpec((1, N_HEADS, tm, AUG), lambda bi, si: (bi, 0, si, 0)),
                   pl.BlockSpec((1, N_HEADS, 1, V_ROWS, tm), lambda bi, si: (bi, 0, si, 0, 0)),
                   tok, tok],
        out_shape=[jax.ShapeDtypeStruct((b, N_HEADS, s // tm, AUG, tm), BF16),
                   jax.ShapeDtypeStruct((b, N_HEADS, s, AUG), BF16),
                   jax.ShapeDtypeStruct((b, N_HEADS, s // tm, V_ROWS, tm), BF16),
                   jax.ShapeDtypeStruct((b, s, d), F32),
                   jax.ShapeDtypeStruct((b, s, d), F32)],
        scratch_shapes=[pltpu.VMEM((8, CONV_WIDTH), F32),
                        pltpu.VMEM((MASK_SLOTS, ATTN_WIDTH), F32)],
        compiler_params=pltpu.CompilerParams(
            dimension_semantics=("arbitrary", "arbitrary"), vmem_limit_bytes=VMEM_LIMIT),
        name="moba_projection",
    )(far_hi, far_lo, x, g, w_in, conv_w, w_conv_out)


def _attn_kernel(qa_ref, qb_ref, k_ref, vt_ref, brow_ref, oa_ref, ob_ref,
                 q_ref, s_ref, tmax_ref, m_ref, acc_ref, bias_ref, *, n_tiles):
    pr = pl.program_id(2)
    n_items = n_tiles + 1
    n_far = n_items - 2 * N_NEAR
    n_heads = qa_ref.shape[1]
    tile_a, tile_b = pr, n_tiles - 1 - pr

    @pl.when((pl.program_id(1) == 0) & (pr == 0))
    def _():
        kb = lax.broadcasted_iota(jnp.int32, (SUPER, SUPER), 0) // MOBA_BLOCK
        qb = lax.broadcasted_iota(jnp.int32, (SUPER, SUPER), 1) // MOBA_BLOCK
        for hh in range(n_heads):
            for delta in range(N_NEAR):
                rows = jnp.broadcast_to(brow_ref[hh, delta], (SUPER, 2 * SUPER))
                tile = pltpu.roll(rows, 0, 1, stride=1, stride_axis=0)[:, :SUPER]
                dist = (SUPER // MOBA_BLOCK) * delta + qb - kb
                bias_ref[hh, delta] = jnp.where(dist >= FAR_DIST, 0.0, tile)
            bias_ref[hh, N_NEAR] = jnp.zeros((SUPER, SUPER), F32)

    def item(w):
        far_b = jnp.minimum(tile_b + 1 - N_NEAR, n_far)
        rest_b = tile_b + 1 - far_b
        u = w - n_far
        is_far = w < n_far
        second = jnp.where(is_far, w < far_b, u < rest_b)
        t = jnp.where(second, jnp.where(is_far, w, far_b + u),
                      jnp.where(is_far, w - far_b, n_far - far_b + u - rest_b))
        return second.astype(jnp.int32), jnp.where(second, tile_b, tile_a), t

    def produce(w, slot, near, heads=None):
        second, qi, t = item(w)
        keys = pl.ds(pl.multiple_of(t * SUPER, SUPER), SUPER)
        for hh in (range(n_heads) if heads is None else heads):
            st = jnp.dot(k_ref[0, hh, keys, :], q_ref[second, hh], preferred_element_type=F32)
            if near:
                st = st + bias_ref[hh, jnp.minimum(qi - t, N_NEAR)]
            s_ref[slot, hh] = st
            tmax_ref[slot, hh] = jnp.max(st, axis=0, keepdims=True)

    def consume(w, slot, heads=None):
        second, _, t = item(w)
        for hh in (range(n_heads) if heads is None else heads):
            m = m_ref[second, hh]
            m_new = jnp.maximum(m, tmax_ref[slot, hh])
            p = jnp.exp2(s_ref[slot, hh] - m_new).astype(BF16)
            pv = jnp.dot(vt_ref[0, hh, t], p, preferred_element_type=F32)
            acc_ref[second, hh] = jnp.exp2(m - m_new) * acc_ref[second, hh] + pv
            m_ref[second, hh] = m_new

    q_ref[0] = qa_ref[0, :, 0]
    q_ref[1] = qb_ref[0, :, 0]
    m_ref[...] = jnp.full(m_ref.shape, -jnp.inf, F32)
    acc_ref[...] = jnp.zeros(acc_ref.shape, F32)
    produce(0, 0, near=False)

    def pair(jj, _, near):
        for hh in range(n_heads):
            produce(2 * jj + 1, 1, near, [hh])
            consume(2 * jj, 0, [hh])
        for hh in range(n_heads):
            produce(2 * jj + 2, 0, near, [hh])
            consume(2 * jj + 1, 1, [hh])
        return _

    far_pairs = (n_far - 1) // 2
    lax.fori_loop(0, far_pairs, functools.partial(pair, near=False), 0, unroll=2)
    lax.fori_loop(far_pairs, n_items // 2, functools.partial(pair, near=True), 0, unroll=2)
    consume(n_items - 1, 0)

    for second, o_ref in enumerate((oa_ref, ob_ref)):
        outs = [(acc_ref[second, hh, :HEAD_DIM] / acc_ref[second, hh, HEAD_DIM:HEAD_DIM + 1]).T
                for hh in range(n_heads)]
        o_ref[0, 0, 0] = jnp.concatenate(outs, axis=-1).astype(o_ref.dtype)


def _attention(qt, k_aug, vt, bias_rows):
    b, _, n_tiles, _, _ = qt.shape
    s = n_tiles * SUPER
    hb = 4
    half = n_tiles // 2
    q_spec = lambda tile: pl.BlockSpec((1, hb, 1, AUG, SUPER), lambda hp, bi, pr: (bi, hp, tile(pr), 0, 0))
    o_spec = lambda tile: pl.BlockSpec((1, 1, 1, SUPER, hb * HEAD_DIM),
                                       lambda hp, bi, pr: (bi, tile(pr), hp, 0, 0))
    o_shape = jax.ShapeDtypeStruct((b, half, N_HEADS // hb, SUPER, hb * HEAD_DIM), BF16)
    return pl.pallas_call(
        functools.partial(_attn_kernel, n_tiles=n_tiles),
        grid=(N_HEADS // hb, b, half),
        in_specs=[q_spec(lambda pr: pr), q_spec(lambda pr: n_tiles - 1 - pr),
                  pl.BlockSpec((1, hb, s, AUG), lambda hp, bi, pr: (bi, hp, 0, 0)),
                  pl.BlockSpec((1, hb, n_tiles, V_ROWS, SUPER), lambda hp, bi, pr: (bi, hp, 0, 0, 0),
                               pipeline_mode=pl.Buffered(1)),
                  pl.BlockSpec((hb, N_NEAR, 1, 2 * SUPER), lambda hp, bi, pr: (hp, 0, 0, 0))],
        out_specs=[o_spec(lambda pr: pr), o_spec(lambda pr: half - 1 - pr)],
        out_shape=[o_shape, o_shape],
        scratch_shapes=[pltpu.VMEM((2, hb, AUG, SUPER), BF16),
                        pltpu.VMEM((2, hb, SUPER, SUPER), F32),
                        pltpu.VMEM((2, hb, 1, SUPER), F32),
                        pltpu.VMEM((2, hb, 1, SUPER), F32),
                        pltpu.VMEM((2, hb, V_ROWS, SUPER), F32),
                        pltpu.VMEM((hb, N_NEAR + 1, SUPER, SUPER), F32)],
        compiler_params=pltpu.CompilerParams(
            dimension_semantics=("arbitrary", "arbitrary", "arbitrary"),
            vmem_limit_bytes=VMEM_LIMIT),
        name="moba_attention",
    )(qt, qt, k_aug, vt, bias_rows)


def _out_kernel(x_ref, att_lo_ref, att_hi_ref, gc_ref, sga_ref, wao_ref, wo_ref, gpost_ref, gpre_ref,
                w1_ref, w2_ref, gmpost_ref, o_ref):
    first_half = pl.program_id(1) < pl.num_programs(1) // 2
    att = jnp.where(first_half, att_lo_ref[0, 0], att_hi_ref[0, 0])
    att = jnp.concatenate([att[i] for i in range(att.shape[0])], axis=-1)
    ya = jnp.dot(att, wao_ref[...], preferred_element_type=F32)
    m = (gc_ref[0] + sga_ref[0] * ya).astype(BF16)
    mix = jnp.dot(m, wo_ref[...], preferred_element_type=F32)
    x1 = x_ref[0] + _rms(mix, gpost_ref[...])
    h2 = _rms(x1, gpre_ref[...]).astype(BF16)
    f = jnp.zeros(x1.shape, F32)
    for ci in range(D_FF // FF_CHUNK):
        cols = slice(ci * FF_CHUNK, (ci + 1) * FF_CHUNK)
        act = jnp.dot(h2, w1_ref[:, cols], preferred_element_type=F32)
        act = jnp.square(jnp.maximum(act, 0.0)).astype(BF16)
        f = f + jnp.dot(act, w2_ref[cols, :], preferred_element_type=F32)
    o_ref[0] = x1 + _rms(f, gmpost_ref[...])


def _output(x, att_lo, att_hi, gc, sga, w_attn_out, w_o, g_post, g_mlp_pre, w1, w2, g_mlp_post):
    b, s, d = x.shape
    tm = TOKEN_TILE
    half = s // tm // 2
    const = lambda shape: pl.BlockSpec(shape, lambda bi, si: (0,) * len(shape),
                                       pipeline_mode=pl.Buffered(1))
    tok = pl.BlockSpec((1, tm, d), lambda bi, si: (bi, si, 0))
    att_spec = lambda tile: pl.BlockSpec((1, 1) + att_lo.shape[2:], lambda bi, si: (bi, tile(si), 0, 0, 0))
    return pl.pallas_call(
        _out_kernel,
        grid=(b, s // tm),
        in_specs=[tok,
                  att_spec(lambda si: jnp.minimum(si, half - 1)),
                  att_spec(lambda si: jnp.maximum(si - half, 0)),
                  tok, tok,
                  const((ATTN_WIDTH, d)), const((d, d)), const((1, d)), const((1, d)),
                  const((d, D_FF)), const((D_FF, d)), const((1, d))],
        out_specs=tok,
        out_shape=jax.ShapeDtypeStruct((b, s, d), F32),
        compiler_params=pltpu.CompilerParams(
            dimension_semantics=("arbitrary", "arbitrary"), vmem_limit_bytes=VMEM_LIMIT),
        name="moba_output_mlp",
    )(x, att_lo, att_hi, gc, sga, w_attn_out, w_o, g_post, g_mlp_pre, w1, w2, g_mlp_post)


def kernel(x, ln_mix_pre, w_in, conv_w, w_conv_out, w_attn_out, rel_bias, w_o, ln_mix_post,
           ln_mlp_pre, w_mlp_in, w_mlp_out, ln_mlp_post):
    depth = w_in.shape[0]
    b, s, d = x.shape
    assert d == D_MODEL and TOKEN_TILE == SUPER and s % (2 * SUPER) == 0 and s // SUPER >= 2 * N_NEAR
    assert MOBA_TOPK <= s // MOBA_BLOCK <= MASK_SLOTS
    bias_rows = _bias_rows(rel_bias)
    far_hi, far_lo = _far_bias(rel_bias)
    for l in range(depth):
        qt, k_aug, vt, gc, sga = _projection(
            x, ln_mix_pre[l][None], w_in[l].astype(BF16), conv_w[l], w_conv_out[l].astype(BF16),
            far_hi, far_lo)
        att_lo, att_hi = _attention(qt, k_aug, vt, bias_rows)
        x = _output(x, att_lo, att_hi, gc, sga, w_attn_out[l].astype(BF16), w_o[l].astype(BF16),
                    ln_mix_post[l][None], ln_mlp_pre[l][None], w_mlp_in[l].astype(BF16),
                    w_mlp_out[l].astype(BF16), ln_mlp_post[l][None])
    return x
```

```python
import functools
import math

import jax
import jax.numpy as jnp
from jax import lax
from jax.experimental import pallas as pl
from jax.experimental.pallas import tpu as pltpu

D_MODEL = 1024
CONV_WIDTH = D_MODEL // 2
CONV_K = 3
N_HEADS = 8
HEAD_DIM = 64
ATTN_WIDTH = N_HEADS * HEAD_DIM
MOBA_BLOCK = 256
MOBA_TOPK = 3
N_BUCKETS = 32
MAX_DISTANCE = 1024
D_FF = 4 * D_MODEL
RMS_EPS = 1e-6
N_IN = 3 * CONV_WIDTH + 3 * ATTN_WIDTH + 2 * D_MODEL
NEG = -1e30
LOG2E = math.log2(math.e)

LANES = 128
BF16_SUBLANES = 16
AUG = 2 * HEAD_DIM
MASK_SLOTS = (AUG - HEAD_DIM) // 2
V_ROWS = HEAD_DIM + BF16_SUBLANES
SUPER = 2 * MOBA_BLOCK
FAR_DIST = -(-(MAX_DISTANCE + MOBA_BLOCK) // MOBA_BLOCK)
N_NEAR = (FAR_DIST + 1) // 2
TOKEN_TILE = 512
FF_CHUNK = 1024
VMEM_LIMIT = 56 * 1024 * 1024

F32 = jnp.float32
BF16 = jnp.bfloat16


def _rms(x, g):
    return (x * lax.rsqrt(jnp.mean(x * x, axis=-1, keepdims=True) + RMS_EPS)) * g


def _t5_causal_bucket(rel):
    n = jnp.maximum(rel, 0)
    max_exact = N_BUCKETS // 2
    nf = jnp.maximum(n, max_exact).astype(F32)
    large = max_exact + (jnp.log(nf / max_exact) / math.log(MAX_DISTANCE / max_exact)
                         * (N_BUCKETS - max_exact)).astype(jnp.int32)
    large = jnp.minimum(large, N_BUCKETS - 1)
    return jnp.where(n < max_exact, n, large)


def _bias_rows(rel_bias):
    m = jnp.arange(2 * SUPER)
    delta = jnp.arange(N_NEAR)[:, None]
    rel = delta * SUPER + jnp.where(m < SUPER, m, m - 2 * SUPER)
    w = jnp.where(rel >= 0, rel_bias[:, _t5_causal_bucket(rel)] * LOG2E, NEG).astype(F32)
    return w[:, :, None, :]


def _far_bias(rel_bias):
    c = rel_bias[:, _t5_causal_bucket(jnp.int32(FAR_DIST * MOBA_BLOCK))] * LOG2E
    hi = c.astype(BF16).astype(F32)
    return hi, c - hi


def _proj_kernel(far_hi_ref, far_lo_ref, x_ref, g_ref, win_ref, cw_ref, wco_ref,
                 qt_ref, k_ref, vt_ref, gc_ref, sga_ref,
                 carry_ref, kmean_ref):
    tm = x_ref.shape[1]
    c, a, d = CONV_WIDTH, ATTN_WIDTH, D_MODEL
    s = pl.program_id(1)
    blocks_per_tile = tm // MOBA_BLOCK
    blk0 = s * blocks_per_tile

    @pl.when(s == 0)
    def _():
        carry_ref[...] = jnp.zeros_like(carry_ref)
        kmean_ref[...] = jnp.zeros_like(kmean_ref)

    h = _rms(x_ref[0], g_ref[...]).astype(BF16)

    def seg(lo, width):
        return jnp.dot(h, win_ref[:, lo:lo + width], preferred_element_type=F32)

    q = seg(3 * c, a)
    k = seg(3 * c + a, a)

    rows_nb = lax.broadcasted_iota(jnp.int32, kmean_ref.shape, 0)
    km = kmean_ref[...]
    for bi in range(blocks_per_tile):
        ks = jnp.sum(k[bi * MOBA_BLOCK:(bi + 1) * MOBA_BLOCK], axis=0, keepdims=True) * (1.0 / MOBA_BLOCK)
        km = jnp.where(rows_nb == blk0 + bi, ks, km)
    kmean_ref[...] = km

    qt = q.T
    n_iota = lax.broadcasted_iota(jnp.int32, (MASK_SLOTS, tm), 0)
    tok_blk = blk0 + lax.broadcasted_iota(jnp.int32, (MASK_SLOTS, tm), 1) // MOBA_BLOCK
    past = n_iota < tok_blk
    far = tok_blk - n_iota >= FAR_DIST
    dh = d // 2
    later = [(3 * c + 2 * a, a), (0, c), (2 * c, c), (c, c),
             (3 * c + 3 * a, dh), (3 * c + 3 * a + dh, dh),
             (3 * c + 3 * a + d, dh), (3 * c + 3 * a + d + dh, dh)]
    assert len(later) == N_HEADS
    segs = []
    for hd in range(N_HEADS):
        qth = qt[hd * HEAD_DIM:(hd + 1) * HEAD_DIM]
        gate = jnp.dot(km[:, hd * HEAD_DIM:(hd + 1) * HEAD_DIM].astype(BF16), qth.astype(BF16),
                       preferred_element_type=F32)
        g = jnp.where(past, gate, NEG)
        sel = n_iota == tok_blk
        for _ in range(MOBA_TOPK):
            m = jnp.max(g, axis=0, keepdims=True)
            idx = jnp.min(jnp.where(g == m, n_iota, MASK_SLOTS), axis=0, keepdims=True)
            pick = n_iota == idx
            sel = sel | (pick & past)
            g = jnp.where(pick, -jnp.inf, g)
        mask_hi = jnp.where(sel, jnp.where(far, far_hi_ref[hd], 0.0), NEG)
        mask_lo = jnp.where(sel & far, far_lo_ref[hd], 0.0)
        qt_ref[0, hd, 0] = jnp.concatenate(
            [qth * (HEAD_DIM ** -0.5 * LOG2E), mask_hi, mask_lo], axis=0).astype(BF16)
        segs.append(seg(*later[hd]))
    v, xin, gcv, gb, gconv0, gconv1, gattn0, gattn1 = segs

    vt = v.T
    ones_row = jnp.where(lax.broadcasted_iota(jnp.int32, (V_ROWS - HEAD_DIM, tm), 0) == 0, 1.0, 0.0)
    for hd in range(N_HEADS):
        vt_ref[0, hd, 0] = jnp.concatenate(
            [vt[hd * HEAD_DIM:(hd + 1) * HEAD_DIM], ones_row], axis=0).astype(BF16)

    lane = lax.broadcasted_iota(jnp.int32, (tm, LANES), 1)
    row_blk = blk0 + lax.broadcasted_iota(jnp.int32, (tm, LANES), 0) // MOBA_BLOCK
    k_pat = jnp.where(lane % MASK_SLOTS == row_blk, 1.0, 0.0)
    low = lane < HEAD_DIM
    for hp in range(N_HEADS // 2):
        kt = k[:, hp * LANES:(hp + 1) * LANES]
        k_ref[0, 2 * hp] = jnp.where(low, kt, k_pat).astype(BF16)
        k_ref[0, 2 * hp + 1] = jnp.where(low, pltpu.roll(kt, HEAD_DIM, 1), k_pat).astype(BF16)

    u = gcv * xin
    row = lax.broadcasted_iota(jnp.int32, u.shape, 0)
    prev = carry_ref[...]
    p1, p2 = prev[7:8], prev[6:7]
    u1 = jnp.where(row == 0, p1, pltpu.roll(u, 1, 0))
    u2 = jnp.where(row == 0, p2, jnp.where(row == 1, p1, pltpu.roll(u, 2, 0)))
    carry_ref[...] = u[tm - 8:]
    cw = cw_ref[...]
    conv = cw[2:3] * u + cw[1:2] * u1 + cw[0:1] * u2
    z = (gb * conv).astype(BF16)
    yc = jnp.dot(z, wco_ref[...], preferred_element_type=F32)
    gc_ref[0, :, :dh] = jax.nn.sigmoid(gconv0) * yc[:, :dh]
    gc_ref[0, :, dh:] = jax.nn.sigmoid(gconv1) * yc[:, dh:]
    sga_ref[0, :, :dh] = jax.nn.sigmoid(gattn0)
    sga_ref[0, :, dh:] = jax.nn.sigmoid(gattn1)


def _projection(x, g, w_in, conv_w, w_conv_out, far_hi, far_lo):
    b, s, d = x.shape
    tm = TOKEN_TILE
    const = lambda shape: pl.BlockSpec(shape, lambda bi, si: (0,) * len(shape),
                                       pipeline_mode=pl.Buffered(1))
    smem = pl.BlockSpec(memory_space=pltpu.SMEM)
    tok = pl.BlockSpec((1, tm, d), lambda bi, si: (bi, si, 0))
    return pl.pallas_call(
        _proj_kernel,
        grid=(b, s // tm),
        in_specs=[smem, smem, tok, const((1, d)), const((d, N_IN)), const((CONV_K, CONV_WIDTH)),
                  const((CONV_WIDTH, d))],
        out_specs=[pl.BlockSpec((1, N_HEADS, 1, AUG, tm), lambda bi, si: (bi, 0, si, 0, 0)),
                   pl.BlockSpec((1, N_HEADS, tm, AUG), lambda bi, si: (bi, 0, si, 0)),
                   pl.BlockSpec((1, N_HEADS, 1, V_ROWS, tm), lambda bi, si: (bi, 0, si, 0, 0)),
                   tok, tok],
        out_shape=[jax.ShapeDtypeStruct((b, N_HEADS, s // tm, AUG, tm), BF16),
                   jax.ShapeDtypeStruct((b, N_HEADS, s, AUG), BF16),
                   jax.ShapeDtypeStruct((b, N_HEADS, s // tm, V_ROWS, tm), BF16),
                   jax.ShapeDtypeStruct((b, s, d), F32),
                   jax.ShapeDtypeStruct((b, s, d), F32)],
        scratch_shapes=[pltpu.VMEM((8, CONV_WIDTH), F32),
                        pltpu.VMEM((MASK_SLOTS, ATTN_WIDTH), F32)],
        compiler_params=pltpu.CompilerParams(
            dimension_semantics=("arbitrary", "arbitrary"), vmem_limit_bytes=VMEM_LIMIT),
        name="moba_projection",
    )(far_hi, far_lo, x, g, w_in, conv_w, w_conv_out)


def _attn_kernel(qa_ref, qb_ref, k_ref, vt_ref, brow_ref, oa_ref, ob_ref,
                 q_ref, s_ref, tmax_ref, m_ref, acc_ref, bias_ref, *, n_tiles):
    pr = pl.program_id(2)
    n_items = n_tiles + 1
    n_far = n_items - 2 * N_NEAR
    n_heads = qa_ref.shape[1]
    tile_a, tile_b = pr, n_tiles - 1 - pr

    zero_tile = n_heads * N_NEAR

    @pl.when((pl.program_id(1) == 0) & (pr == 0))
    def _():
        kb = lax.broadcasted_iota(jnp.int32, (SUPER, SUPER), 0) // MOBA_BLOCK
        qb = lax.broadcasted_iota(jnp.int32, (SUPER, SUPER), 1) // MOBA_BLOCK
        for hh in range(n_heads):
            for delta in range(N_NEAR):
                rows = jnp.broadcast_to(brow_ref[hh, delta], (SUPER, 2 * SUPER))
                tile = pltpu.roll(rows, 0, 1, stride=1, stride_axis=0)[:, :SUPER]
                dist = (SUPER // MOBA_BLOCK) * delta + qb - kb
                bias_ref[hh * N_NEAR + delta] = jnp.where(dist >= FAR_DIST, 0.0, tile)
        bias_ref[zero_tile] = jnp.zeros((SUPER, SUPER), F32)

    def item(w):
        far_b = jnp.minimum(tile_b + 1 - N_NEAR, n_far)
        rest_b = tile_b + 1 - far_b
        u = w - n_far
        is_far = w < n_far
        second = jnp.where(is_far, w < far_b, u < rest_b)
        t = jnp.where(second, jnp.where(is_far, w, far_b + u),
                      jnp.where(is_far, w - far_b, n_far - far_b + u - rest_b))
        return second.astype(jnp.int32), jnp.where(second, tile_b, tile_a), t

    def produce(w, slot, near, heads=None):
        second, qi, t = item(w)
        keys = pl.ds(pl.multiple_of(t * SUPER, SUPER), SUPER)
        for hh in (range(n_heads) if heads is None else heads):
            st = jnp.dot(k_ref[0, hh, keys, :], q_ref[second, hh], preferred_element_type=F32)
            if near:
                delta = qi - t
                st = st + bias_ref[jnp.where(delta < N_NEAR, hh * N_NEAR + delta, zero_tile)]
            s_ref[slot, hh] = st
            tmax_ref[slot, hh] = jnp.max(st, axis=0, keepdims=True)

    def consume(w, slot, heads=None):
        second, _, t = item(w)
        for hh in (range(n_heads) if heads is None else heads):
            m = m_ref[second, hh]
            m_new = jnp.maximum(m, tmax_ref[slot, hh])
            p = jnp.exp2(s_ref[slot, hh] - m_new).astype(BF16)
            pv = jnp.dot(vt_ref[0, hh, t], p, preferred_element_type=F32)
            acc_ref[second, hh] = jnp.exp2(m - m_new) * acc_ref[second, hh] + pv
            m_ref[second, hh] = m_new

    q_ref[0] = qa_ref[0, :, 0]
    q_ref[1] = qb_ref[0, :, 0]
    m_ref[...] = jnp.full(m_ref.shape, -jnp.inf, F32)
    acc_ref[...] = jnp.zeros(acc_ref.shape, F32)
    produce(0, 0, near=False)

    def pair(jj, _, near):
        for hh in range(n_heads):
            produce(2 * jj + 1, 1, near, [hh])
            consume(2 * jj, 0, [hh])
        for hh in range(n_heads):
            produce(2 * jj + 2, 0, near, [hh])
            consume(2 * jj + 1, 1, [hh])
        return _

    far_pairs = (n_far - 1) // 2
    lax.fori_loop(0, far_pairs, functools.partial(pair, near=False), 0, unroll=2)
    lax.fori_loop(far_pairs, n_items // 2, functools.partial(pair, near=True), 0, unroll=2)
    consume(n_items - 1, 0)

    for second, o_ref in enumerate((oa_ref, ob_ref)):
        outs = [(acc_ref[second, hh, :HEAD_DIM] / acc_ref[second, hh, HEAD_DIM:HEAD_DIM + 1]).T
                for hh in range(n_heads)]
        o_ref[0, 0, 0] = jnp.concatenate(outs, axis=-1).astype(o_ref.dtype)


def _attention(qt, k_aug, vt, bias_rows):
    b, _, n_tiles, _, _ = qt.shape
    s = n_tiles * SUPER
    hb = 4
    half = n_tiles // 2
    q_spec = lambda tile: pl.BlockSpec((1, hb, 1, AUG, SUPER), lambda hp, bi, pr: (bi, hp, tile(pr), 0, 0))
    o_spec = lambda tile: pl.BlockSpec((1, 1, 1, SUPER, hb * HEAD_DIM),
                                       lambda hp, bi, pr: (bi, tile(pr), hp, 0, 0))
    o_shape = jax.ShapeDtypeStruct((b, half, N_HEADS // hb, SUPER, hb * HEAD_DIM), BF16)
    return pl.pallas_call(
        functools.partial(_attn_kernel, n_tiles=n_tiles),
        grid=(N_HEADS // hb, b, half),
        in_specs=[q_spec(lambda pr: pr), q_spec(lambda pr: n_tiles - 1 - pr),
                  pl.BlockSpec((1, hb, s, AUG), lambda hp, bi, pr: (bi, hp, 0, 0)),
                  pl.BlockSpec((1, hb, n_tiles, V_ROWS, SUPER), lambda hp, bi, pr: (bi, hp, 0, 0, 0)),
                  pl.BlockSpec((hb, N_NEAR, 1, 2 * SUPER), lambda hp, bi, pr: (hp, 0, 0, 0))],
        out_specs=[o_spec(lambda pr: pr), o_spec(lambda pr: half - 1 - pr)],
        out_shape=[o_shape, o_shape],
        scratch_shapes=[pltpu.VMEM((2, hb, AUG, SUPER), BF16),
                        pltpu.VMEM((2, hb, SUPER, SUPER), F32),
                        pltpu.VMEM((2, hb, 1, SUPER), F32),
                        pltpu.VMEM((2, hb, 1, SUPER), F32),
                        pltpu.VMEM((2, hb, V_ROWS, SUPER), F32),
                        pltpu.VMEM((hb * N_NEAR + 1, SUPER, SUPER), F32)],
        compiler_params=pltpu.CompilerParams(
            dimension_semantics=("arbitrary", "arbitrary", "arbitrary"),
            vmem_limit_bytes=VMEM_LIMIT),
        name="moba_attention",
    )(qt, qt, k_aug, vt, bias_rows)


def _out_kernel(x_ref, att_lo_ref, att_hi_ref, gc_ref, sga_ref, wao_ref, wo_ref, gpost_ref, gpre_ref,
                w1_ref, w2_ref, gmpost_ref, o_ref):
    first_half = pl.program_id(1) < pl.num_programs(1) // 2
    att = jnp.where(first_half, att_lo_ref[0, 0], att_hi_ref[0, 0])
    att = jnp.concatenate([att[i] for i in range(att.shape[0])], axis=-1)
    ya = jnp.dot(att, wao_ref[...], preferred_element_type=F32)
    m = (gc_ref[0] + sga_ref[0] * ya).astype(BF16)
    mix = jnp.dot(m, wo_ref[...], preferred_element_type=F32)
    x1 = x_ref[0] + _rms(mix, gpost_ref[...])
    h2 = _rms(x1, gpre_ref[...]).astype(BF16)
    f = jnp.zeros(x1.shape, F32)
    for ci in range(D_FF // FF_CHUNK):
        cols = slice(ci * FF_CHUNK, (ci + 1) * FF_CHUNK)
        act = jnp.dot(h2, w1_ref[:, cols], preferred_element_type=F32)
        act = jnp.square(jnp.maximum(act, 0.0)).astype(BF16)
        f = f + jnp.dot(act, w2_ref[cols, :], preferred_element_type=F32)
    o_ref[0] = x1 + _rms(f, gmpost_ref[...])


def _output(x, att_lo, att_hi, gc, sga, w_attn_out, w_o, g_post, g_mlp_pre, w1, w2, g_mlp_post):
    b, s, d = x.shape
    tm = TOKEN_TILE
    half = s // tm // 2
    const = lambda shape: pl.BlockSpec(shape, lambda bi, si: (0,) * len(shape),
                                       pipeline_mode=pl.Buffered(1))
    tok = pl.BlockSpec((1, tm, d), lambda bi, si: (bi, si, 0))
    att_spec = lambda tile: pl.BlockSpec((1, 1) + att_lo.shape[2:], lambda bi, si: (bi, tile(si), 0, 0, 0))
    return pl.pallas_call(
        _out_kernel,
        grid=(b, s // tm),
        in_specs=[tok,
                  att_spec(lambda si: jnp.minimum(si, half - 1)),
                  att_spec(lambda si: jnp.maximum(si - half, 0)),
                  tok, tok,
                  const((ATTN_WIDTH, d)), const((d, d)), const((1, d)), const((1, d)),
                  const((d, D_FF)), const((D_FF, d)), const((1, d))],
        out_specs=tok,
        out_shape=jax.ShapeDtypeStruct((b, s, d), F32),
        compiler_params=pltpu.CompilerParams(
            dimension_semantics=("arbitrary", "arbitrary"), vmem_limit_bytes=VMEM_LIMIT),
        name="moba_output_mlp",
    )(x, att_lo, att_hi, gc, sga, w_attn_out, w_o, g_post, g_mlp_pre, w1, w2, g_mlp_post)


def kernel(x, ln_mix_pre, w_in, conv_w, w_conv_out, w_attn_out, rel_bias, w_o, ln_mix_post,
           ln_mlp_pre, w_mlp_in, w_mlp_out, ln_mlp_post):
    depth = w_in.shape[0]
    b, s, d = x.shape
    assert d == D_MODEL and TOKEN_TILE == SUPER and s % (2 * SUPER) == 0 and s // SUPER >= 2 * N_NEAR
    assert MOBA_TOPK <= s // MOBA_BLOCK <= MASK_SLOTS
    bias_rows = _bias_rows(rel_bias)
    far_hi, far_lo = _far_bias(rel_bias)
    for l in range(depth):
        qt, k_aug, vt, gc, sga = _projection(
            x, ln_mix_pre[l][None], w_in[l].astype(BF16), conv_w[l], w_conv_out[l].astype(BF16),
            far_hi, far_lo)
        att_lo, att_hi = _attention(qt, k_aug, vt, bias_rows)
        x = _output(x, att_lo, att_hi, gc, sga, w_attn_out[l].astype(BF16), w_o[l].astype(BF16),
                    ln_mix_post[l][None], ln_mlp_pre[l][None], w_mlp_in[l].astype(BF16),
                    w_mlp_out[l].astype(BF16), ln_mlp_post[l][None])
    return x
```

```python
import functools
import math

import jax
import jax.numpy as jnp
from jax import lax
from jax.experimental import pallas as pl
from jax.experimental.pallas import tpu as pltpu

D_MODEL = 1024
CONV_WIDTH = D_MODEL // 2
CONV_K = 3
N_HEADS = 8
HEAD_DIM = 64
ATTN_WIDTH = N_HEADS * HEAD_DIM
MOBA_BLOCK = 256
MOBA_TOPK = 3
N_BUCKETS = 32
MAX_DISTANCE = 1024
D_FF = 4 * D_MODEL
RMS_EPS = 1e-6
N_IN = 3 * CONV_WIDTH + 3 * ATTN_WIDTH + 2 * D_MODEL
NEG = -1e30
LOG2E = math.log2(math.e)

LANES = 128
BF16_SUBLANES = 16
AUG = 2 * HEAD_DIM
MASK_SLOTS = (AUG - HEAD_DIM) // 2
V_ROWS = HEAD_DIM + BF16_SUBLANES
SUPER = 2 * MOBA_BLOCK
FAR_DIST = -(-(MAX_DISTANCE + MOBA_BLOCK) // MOBA_BLOCK)
N_NEAR = (FAR_DIST + 1) // 2
TOKEN_TILE = 512
FF_CHUNK = 1024
VMEM_LIMIT = 56 * 1024 * 1024

F32 = jnp.float32
BF16 = jnp.bfloat16


def _rms(x, g):
    return (x * lax.rsqrt(jnp.mean(x * x, axis=-1, keepdims=True) + RMS_EPS)) * g


def _t5_causal_bucket(rel):
    n = jnp.maximum(rel, 0)
    max_exact = N_BUCKETS // 2
    nf = jnp.maximum(n, max_exact).astype(F32)
    large = max_exact + (jnp.log(nf / max_exact) / math.log(MAX_DISTANCE / max_exact)
                         * (N_BUCKETS - max_exact)).astype(jnp.int32)
    large = jnp.minimum(large, N_BUCKETS - 1)
    return jnp.where(n < max_exact, n, large)


def _bias_rows(rel_bias):
    m = jnp.arange(2 * SUPER)
    delta = jnp.arange(N_NEAR)[:, None]
    rel = delta * SUPER + jnp.where(m < SUPER, m, m - 2 * SUPER)
    w = jnp.where(rel >= 0, rel_bias[:, _t5_causal_bucket(rel)] * LOG2E, NEG).astype(F32)
    return w[:, :, None, :]


def _far_bias(rel_bias):
    c = rel_bias[:, _t5_causal_bucket(jnp.int32(FAR_DIST * MOBA_BLOCK))] * LOG2E
    hi = c.astype(BF16).astype(F32)
    return hi, c - hi


def _proj_kernel(far_hi_ref, far_lo_ref, x_ref, g_ref, win_ref, cw_ref, wco_ref,
                 qt_ref, k_ref, vt_ref, gc_ref, sga_ref,
                 carry_ref, kmean_ref):
    tm = x_ref.shape[1]
    c, a, d = CONV_WIDTH, ATTN_WIDTH, D_MODEL
    s = pl.program_id(1)
    blocks_per_tile = tm // MOBA_BLOCK
    blk0 = s * blocks_per_tile

    @pl.when(s == 0)
    def _():
        carry_ref[...] = jnp.zeros_like(carry_ref)
        kmean_ref[...] = jnp.zeros_like(kmean_ref)

    h = _rms(x_ref[0], g_ref[...]).astype(BF16)

    def seg(lo, width):
        return jnp.dot(h, win_ref[:, lo:lo + width], preferred_element_type=F32)

    u = seg(2 * c, c) * seg(0, c)
    row = lax.broadcasted_iota(jnp.int32, u.shape, 0)
    prev = carry_ref[...]
    p1, p2 = prev[7:8], prev[6:7]
    u1 = jnp.where(row == 0, p1, pltpu.roll(u, 1, 0))
    u2 = jnp.where(row == 0, p2, jnp.where(row == 1, p1, pltpu.roll(u, 2, 0)))
    carry_ref[...] = u[tm - 8:]
    cw = cw_ref[...]
    conv = cw[2:3] * u + cw[1:2] * u1 + cw[0:1] * u2
    z = (seg(c, c) * conv).astype(BF16)
    yc = jnp.dot(z, wco_ref[...], preferred_element_type=F32)
    gc_ref[0] = jax.nn.sigmoid(seg(3 * c + 3 * a, d)) * yc
    sga_ref[0] = jax.nn.sigmoid(seg(3 * c + 3 * a + d, d))

    q = seg(3 * c, a)
    k = seg(3 * c + a, a)
    v = seg(3 * c + 2 * a, a)

    rows_nb = lax.broadcasted_iota(jnp.int32, kmean_ref.shape, 0)
    km = kmean_ref[...]
    for bi in range(blocks_per_tile):
        ks = jnp.sum(k[bi * MOBA_BLOCK:(bi + 1) * MOBA_BLOCK], axis=0, keepdims=True) * (1.0 / MOBA_BLOCK)
        km = jnp.where(rows_nb == blk0 + bi, ks, km)
    kmean_ref[...] = km

    qt = q.T
    vt = v.T
    n_iota = lax.broadcasted_iota(jnp.int32, (MASK_SLOTS, tm), 0)
    tok_blk = blk0 + lax.broadcasted_iota(jnp.int32, (MASK_SLOTS, tm), 1) // MOBA_BLOCK
    past = n_iota < tok_blk
    far = tok_blk - n_iota >= FAR_DIST
    ones_row = jnp.where(lax.broadcasted_iota(jnp.int32, (V_ROWS - HEAD_DIM, tm), 0) == 0, 1.0, 0.0)
    for hd in range(N_HEADS):
        qth = qt[hd * HEAD_DIM:(hd + 1) * HEAD_DIM]
        gate = jnp.dot(km[:, hd * HEAD_DIM:(hd + 1) * HEAD_DIM].astype(BF16), qth.astype(BF16),
                       preferred_element_type=F32)
        g = jnp.where(past, gate, NEG)
        sel = n_iota == tok_blk
        for _ in range(MOBA_TOPK):
            m = jnp.max(g, axis=0, keepdims=True)
            idx = jnp.min(jnp.where(g == m, n_iota, MASK_SLOTS), axis=0, keepdims=True)
            pick = n_iota == idx
            sel = sel | (pick & past)
            g = jnp.where(pick, -jnp.inf, g)
        mask_hi = jnp.where(sel, jnp.where(far, far_hi_ref[hd], 0.0), NEG)
        mask_lo = jnp.where(sel & far, far_lo_ref[hd], 0.0)
        qt_ref[0, hd, 0] = jnp.concatenate(
            [qth * (HEAD_DIM ** -0.5 * LOG2E), mask_hi, mask_lo], axis=0).astype(BF16)
        vt_ref[0, hd, 0] = jnp.concatenate(
            [vt[hd * HEAD_DIM:(hd + 1) * HEAD_DIM], ones_row], axis=0).astype(BF16)

    lane = lax.broadcasted_iota(jnp.int32, (tm, LANES), 1)
    row_blk = blk0 + lax.broadcasted_iota(jnp.int32, (tm, LANES), 0) // MOBA_BLOCK
    k_pat = jnp.where(lane % MASK_SLOTS == row_blk, 1.0, 0.0)
    low = lane < HEAD_DIM
    for hp in range(N_HEADS // 2):
        kt = k[:, hp * LANES:(hp + 1) * LANES]
        k_ref[0, 2 * hp] = jnp.where(low, kt, k_pat).astype(BF16)
        k_ref[0, 2 * hp + 1] = jnp.where(low, pltpu.roll(kt, HEAD_DIM, 1), k_pat).astype(BF16)


def _projection(x, g, w_in, conv_w, w_conv_out, far_hi, far_lo):
    b, s, d = x.shape
    tm = TOKEN_TILE
    const = lambda shape: pl.BlockSpec(shape, lambda bi, si: (0,) * len(shape),
                                       pipeline_mode=pl.Buffered(1))
    smem = pl.BlockSpec(memory_space=pltpu.SMEM)
    tok = pl.BlockSpec((1, tm, d), lambda bi, si: (bi, si, 0))
    return pl.pallas_call(
        _proj_kernel,
        grid=(b, s // tm),
        in_specs=[smem, smem, tok, const((1, d)), const((d, N_IN)), const((CONV_K, CONV_WIDTH)),
                  const((CONV_WIDTH, d))],
        out_specs=[pl.BlockSpec((1, N_HEADS, 1, AUG, tm), lambda bi, si: (bi, 0, si, 0, 0)),
                   pl.BlockSpec((1, N_HEADS, tm, AUG), lambda bi, si: (bi, 0, si, 0)),
                   pl.BlockSpec((1, N_HEADS, 1, V_ROWS, tm), lambda bi, si: (bi, 0, si, 0, 0)),
                   tok, tok],
        out_shape=[jax.ShapeDtypeStruct((b, N_HEADS, s // tm, AUG, tm), BF16),
                   jax.ShapeDtypeStruct((b, N_HEADS, s, AUG), BF16),
                   jax.ShapeDtypeStruct((b, N_HEADS, s // tm, V_ROWS, tm), BF16),
                   jax.ShapeDtypeStruct((b, s, d), F32),
                   jax.ShapeDtypeStruct((b, s, d), F32)],
        scratch_shapes=[pltpu.VMEM((8, CONV_WIDTH), F32),
                        pltpu.VMEM((MASK_SLOTS, ATTN_WIDTH), F32)],
        compiler_params=pltpu.CompilerParams(
            dimension_semantics=("arbitrary", "arbitrary"), vmem_limit_bytes=VMEM_LIMIT),
        name="moba_projection",
    )(far_hi, far_lo, x, g, w_in, conv_w, w_conv_out)


def _attn_kernel(qa_ref, qb_ref, k_ref, vt_ref, brow_ref, oa_ref, ob_ref,
                 q_ref, s_ref, tmax_ref, m_ref, acc_ref, bias_ref, *, n_tiles):
    pr = pl.program_id(2)
    n_items = n_tiles + 1
    n_far = n_items - 2 * N_NEAR
    n_heads = qa_ref.shape[1]
    tile_a, tile_b = pr, n_tiles - 1 - pr

    zero_tile = n_heads * N_NEAR

    @pl.when((pl.program_id(1) == 0) & (pr == 0))
    def _():
        kb = lax.broadcasted_iota(jnp.int32, (SUPER, SUPER), 0) // MOBA_BLOCK
        qb = lax.broadcasted_iota(jnp.int32, (SUPER, SUPER), 1) // MOBA_BLOCK
        for hh in range(n_heads):
            for delta in range(N_NEAR):
                rows = jnp.broadcast_to(brow_ref[hh, delta], (SUPER, 2 * SUPER))
                tile = pltpu.roll(rows, 0, 1, stride=1, stride_axis=0)[:, :SUPER]
                dist = (SUPER // MOBA_BLOCK) * delta + qb - kb
                bias_ref[hh * N_NEAR + delta] = jnp.where(dist >= FAR_DIST, 0.0, tile)
        bias_ref[zero_tile] = jnp.zeros((SUPER, SUPER), F32)

    def item(w):
        far_b = jnp.minimum(tile_b + 1 - N_NEAR, n_far)
        rest_b = tile_b + 1 - far_b
        u = w - n_far
        is_far = w < n_far
        second = jnp.where(is_far, w < far_b, u < rest_b)
        t = jnp.where(second, jnp.where(is_far, w, far_b + u),
                      jnp.where(is_far, w - far_b, n_far - far_b + u - rest_b))
        return second.astype(jnp.int32), jnp.where(second, tile_b, tile_a), t

    def produce(w, slot, near, heads=None):
        second, qi, t = item(w)
        keys = pl.ds(pl.multiple_of(t * SUPER, SUPER), SUPER)
        for hh in (range(n_heads) if heads is None else heads):
            st = jnp.dot(k_ref[0, hh, keys, :], q_ref[second, hh], preferred_element_type=F32)
            if near:
                delta = qi - t
                st = st + bias_ref[jnp.where(delta < N_NEAR, hh * N_NEAR + delta, zero_tile)]
            s_ref[slot, hh] = st
            tmax_ref[slot, hh] = jnp.max(st, axis=0, keepdims=True)

    def consume(w, slot, heads=None):
        second, _, t = item(w)
        for hh in (range(n_heads) if heads is None else heads):
            m = m_ref[second, hh]
            m_new = jnp.maximum(m, tmax_ref[slot, hh])
            p = jnp.exp2(s_ref[slot, hh] - m_new).astype(BF16)
            pv = jnp.dot(vt_ref[0, hh, t], p, preferred_element_type=F32)
            acc_ref[second, hh] = jnp.exp2(m - m_new) * acc_ref[second, hh] + pv
            m_ref[second, hh] = m_new

    q_ref[0] = qa_ref[0, :, 0]
    q_ref[1] = qb_ref[0, :, 0]
    m_ref[...] = jnp.full(m_ref.shape, -jnp.inf, F32)
    acc_ref[...] = jnp.zeros(acc_ref.shape, F32)
    produce(0, 0, near=False)

    def pair(jj, _, near):
        for hh in range(n_heads):
            produce(2 * jj + 1, 1, near, [hh])
            consume(2 * jj, 0, [hh])
        for hh in range(n_heads):
            produce(2 * jj + 2, 0, near, [hh])
            consume(2 * jj + 1, 1, [hh])
        return _

    far_pairs = (n_far - 1) // 2
    lax.fori_loop(0, far_pairs, functools.partial(pair, near=False), 0, unroll=2)
    lax.fori_loop(far_pairs, n_items // 2, functools.partial(pair, near=True), 0, unroll=2)
    consume(n_items - 1, 0)

    for second, o_ref in enumerate((oa_ref, ob_ref)):
        outs = [(acc_ref[second, hh, :HEAD_DIM] / acc_ref[second, hh, HEAD_DIM:HEAD_DIM + 1]).T
                for hh in range(n_heads)]
        o_ref[0, 0, 0] = jnp.concatenate(outs, axis=-1).astype(o_ref.dtype)


def _attention(qt, k_aug, vt, bias_rows):
    b, _, n_tiles, _, _ = qt.shape
    s = n_tiles * SUPER
    hb = 4
    half = n_tiles // 2
    q_spec = lambda tile: pl.BlockSpec((1, hb, 1, AUG, SUPER), lambda hp, bi, pr: (bi, hp, tile(pr), 0, 0))
    o_spec = lambda tile: pl.BlockSpec((1, 1, 1, SUPER, hb * HEAD_DIM),
                                       lambda hp, bi, pr: (bi, tile(pr), hp, 0, 0))
    o_shape = jax.ShapeDtypeStruct((b, half, N_HEADS // hb, SUPER, hb * HEAD_DIM), BF16)
    return pl.pallas_call(
        functools.partial(_attn_kernel, n_tiles=n_tiles),
        grid=(N_HEADS // hb, b, half),
        in_specs=[q_spec(lambda pr: pr), q_spec(lambda pr: n_tiles - 1 - pr),
                  pl.BlockSpec((1, hb, s, AUG), lambda hp, bi, pr: (bi, hp, 0, 0)),
                  pl.BlockSpec((1, hb, n_tiles, V_ROWS, SUPER), lambda hp, bi, pr: (bi, hp, 0, 0, 0)),
                  pl.BlockSpec((hb, N_NEAR, 1, 2 * SUPER), lambda hp, bi, pr: (hp, 0, 0, 0))],
        out_specs=[o_spec(lambda pr: pr), o_spec(lambda pr: half - 1 - pr)],
        out_shape=[o_shape, o_shape],
        scratch_shapes=[pltpu.VMEM((2, hb, AUG, SUPER), BF16),
                        pltpu.VMEM((2, hb, SUPER, SUPER), F32),
                        pltpu.VMEM((2, hb, 1, SUPER), F32),
                        pltpu.VMEM((2, hb, 1, SUPER), F32),
                        pltpu.VMEM((2, hb, V_ROWS, SUPER), F32),
                        pltpu.VMEM((hb * N_NEAR + 1, SUPER, SUPER), F32)],
        compiler_params=pltpu.CompilerParams(
            dimension_semantics=("arbitrary", "arbitrary", "arbitrary"),
            vmem_limit_bytes=VMEM_LIMIT),
        name="moba_attention",
    )(qt, qt, k_aug, vt, bias_rows)


def _out_kernel(x_ref, att_lo_ref, att_hi_ref, gc_ref, sga_ref, wao_ref, wo_ref, gpost_ref, gpre_ref,
                w1_ref, w2_ref, gmpost_ref, o_ref):
    first_half = pl.program_id(1) < pl.num_programs(1) // 2
    att = jnp.where(first_half, att_lo_ref[0, 0], att_hi_ref[0, 0])
    att = jnp.concatenate([att[i] for i in range(att.shape[0])], axis=-1)
    ya = jnp.dot(att, wao_ref[...], preferred_element_type=F32)
    m = (gc_ref[0] + sga_ref[0] * ya).astype(BF16)
    mix = jnp.dot(m, wo_ref[...], preferred_element_type=F32)
    x1 = x_ref[0] + _rms(mix, gpost_ref[...])
    h2 = _rms(x1, gpre_ref[...]).astype(BF16)
    f = jnp.zeros(x1.shape, F32)
    for ci in range(D_FF // FF_CHUNK):
        cols = slice(ci * FF_CHUNK, (ci + 1) * FF_CHUNK)
        act = jnp.dot(h2, w1_ref[:, cols], preferred_element_type=F32)
        act = jnp.square(jnp.maximum(act, 0.0)).astype(BF16)
        f = f + jnp.dot(act, w2_ref[cols, :], preferred_element_type=F32)
    o_ref[0] = x1 + _rms(f, gmpost_ref[...])


def _output(x, att_lo, att_hi, gc, sga, w_attn_out, w_o, g_post, g_mlp_pre, w1, w2, g_mlp_post):
    b, s, d = x.shape
    tm = TOKEN_TILE
    half = s // tm // 2
    const = lambda shape: pl.BlockSpec(shape, lambda bi, si: (0,) * len(shape),
                                       pipeline_mode=pl.Buffered(1))
    tok = pl.BlockSpec((1, tm, d), lambda bi, si: (bi, si, 0))
    att_spec = lambda tile: pl.BlockSpec((1, 1) + att_lo.shape[2:], lambda bi, si: (bi, tile(si), 0, 0, 0))
    return pl.pallas_call(
        _out_kernel,
        grid=(b, s // tm),
        in_specs=[tok,
                  att_spec(lambda si: jnp.minimum(si, half - 1)),
                  att_spec(lambda si: jnp.maximum(si - half, 0)),
                  tok, tok,
                  const((ATTN_WIDTH, d)), const((d, d)), const((1, d)), const((1, d)),
                  const((d, D_FF)), const((D_FF, d)), const((1, d))],
        out_specs=tok,
        out_shape=jax.ShapeDtypeStruct((b, s, d), F32),
        compiler_params=pltpu.CompilerParams(
            dimension_semantics=("arbitrary", "arbitrary"), vmem_limit_bytes=VMEM_LIMIT),
        name="moba_output_mlp",
    )(x, att_lo, att_hi, gc, sga, w_attn_out, w_o, g_post, g_mlp_pre, w1, w2, g_mlp_post)


def kernel(x, ln_mix_pre, w_in, conv_w, w_conv_out, w_attn_out, rel_bias, w_o, ln_mix_post,
           ln_mlp_pre, w_mlp_in, w_mlp_out, ln_mlp_post):
    depth = w_in.shape[0]
    b, s, d = x.shape
    assert d == D_MODEL and TOKEN_TILE == SUPER and s % (2 * SUPER) == 0 and s // SUPER >= 2 * N_NEAR
    assert MOBA_TOPK <= s // MOBA_BLOCK <= MASK_SLOTS
    bias_rows = _bias_rows(rel_bias)
    far_hi, far_lo = _far_bias(rel_bias)
    for l in range(depth):
        qt, k_aug, vt, gc, sga = _projection(
            x, ln_mix_pre[l][None], w_in[l].astype(BF16), conv_w[l], w_conv_out[l].astype(BF16),
            far_hi, far_lo)
        att_lo, att_hi = _attention(qt, k_aug, vt, bias_rows)
        x = _output(x, att_lo, att_hi, gc, sga, w_attn_out[l].astype(BF16), w_o[l].astype(BF16),
                    ln_mix_post[l][None], ln_mlp_pre[l][None], w_mlp_in[l].astype(BF16),
                    w_mlp_out[l].astype(BF16), ln_mlp_post[l][None])
    return x
```

```python
import functools
import math

import jax
import jax.numpy as jnp
from jax import lax
from jax.experimental import pallas as pl
from jax.experimental.pallas import tpu as pltpu

D_MODEL = 1024
CONV_WIDTH = D_MODEL // 2
CONV_K = 3
N_HEADS = 8
HEAD_DIM = 64
ATTN_WIDTH = N_HEADS * HEAD_DIM
MOBA_BLOCK = 256
MOBA_TOPK = 3
N_BUCKETS = 32
MAX_DISTANCE = 1024
D_FF = 4 * D_MODEL
RMS_EPS = 1e-6
N_IN = 3 * CONV_WIDTH + 3 * ATTN_WIDTH + 2 * D_MODEL
NEG = -1e30
LOG2E = math.log2(math.e)

LANES = 128
BF16_SUBLANES = 16
AUG = 2 * HEAD_DIM
MASK_SLOTS = (AUG - HEAD_DIM) // 2
V_ROWS = HEAD_DIM + BF16_SUBLANES
SUPER = 2 * MOBA_BLOCK
FAR_DIST = -(-(MAX_DISTANCE + MOBA_BLOCK) // MOBA_BLOCK)
N_NEAR = (FAR_DIST + 1) // 2
TOKEN_TILE = 512
FF_CHUNK = 1024
VMEM_LIMIT = 56 * 1024 * 1024

F32 = jnp.float32
BF16 = jnp.bfloat16


def _rms(x, g):
    return (x * lax.rsqrt(jnp.mean(x * x, axis=-1, keepdims=True) + RMS_EPS)) * g


def _t5_causal_bucket(rel):
    n = jnp.maximum(rel, 0)
    max_exact = N_BUCKETS // 2
    nf = jnp.maximum(n, max_exact).astype(F32)
    large = max_exact + (jnp.log(nf / max_exact) / math.log(MAX_DISTANCE / max_exact)
                         * (N_BUCKETS - max_exact)).astype(jnp.int32)
    large = jnp.minimum(large, N_BUCKETS - 1)
    return jnp.where(n < max_exact, n, large)


def _bias_rows(rel_bias):
    m = jnp.arange(2 * SUPER)
    delta = jnp.arange(N_NEAR)[:, None]
    rel = delta * SUPER + jnp.where(m < SUPER, m, m - 2 * SUPER)
    far = rel_bias[:, _t5_causal_bucket(jnp.int32(FAR_DIST * MOBA_BLOCK))]
    w = jnp.where(rel >= 0, (rel_bias[:, _t5_causal_bucket(rel)] - far[:, None, None]) * LOG2E, NEG)
    return w.astype(F32)[:, :, None, :]


def _proj_kernel(x_ref, g_ref, win_ref, cw_ref, wco_ref,
                 qt_ref, k_ref, vt_ref, gc_ref, sga_ref,
                 carry_ref, kmean_ref):
    tm = x_ref.shape[1]
    c, a, d = CONV_WIDTH, ATTN_WIDTH, D_MODEL
    s = pl.program_id(1)
    blocks_per_tile = tm // MOBA_BLOCK
    blk0 = s * blocks_per_tile

    @pl.when(s == 0)
    def _():
        carry_ref[...] = jnp.zeros_like(carry_ref)
        kmean_ref[...] = jnp.zeros_like(kmean_ref)

    h = _rms(x_ref[0], g_ref[...]).astype(BF16)

    def seg(lo, width):
        return jnp.dot(h, win_ref[:, lo:lo + width], preferred_element_type=F32)

    u = seg(2 * c, c) * seg(0, c)
    row = lax.broadcasted_iota(jnp.int32, u.shape, 0)
    prev = carry_ref[...]
    p1, p2 = prev[7:8], prev[6:7]
    u1 = jnp.where(row == 0, p1, pltpu.roll(u, 1, 0))
    u2 = jnp.where(row == 0, p2, jnp.where(row == 1, p1, pltpu.roll(u, 2, 0)))
    carry_ref[...] = u[tm - 8:]
    cw = cw_ref[...]
    conv = cw[2:3] * u + cw[1:2] * u1 + cw[0:1] * u2
    z = (seg(c, c) * conv).astype(BF16)
    yc = jnp.dot(z, wco_ref[...], preferred_element_type=F32)
    gc_ref[0] = jax.nn.sigmoid(seg(3 * c + 3 * a, d)) * yc
    sga_ref[0] = jax.nn.sigmoid(seg(3 * c + 3 * a + d, d))

    q = seg(3 * c, a)
    k = seg(3 * c + a, a)
    v = seg(3 * c + 2 * a, a)

    rows_nb = lax.broadcasted_iota(jnp.int32, kmean_ref.shape, 0)
    km = kmean_ref[...]
    for bi in range(blocks_per_tile):
        ks = jnp.sum(k[bi * MOBA_BLOCK:(bi + 1) * MOBA_BLOCK], axis=0, keepdims=True) * (1.0 / MOBA_BLOCK)
        km = jnp.where(rows_nb == blk0 + bi, ks, km)
    kmean_ref[...] = km

    qt = q.T
    vt = v.T
    n_iota = lax.broadcasted_iota(jnp.int32, (MASK_SLOTS, tm), 0)
    tok_blk = blk0 + lax.broadcasted_iota(jnp.int32, (MASK_SLOTS, tm), 1) // MOBA_BLOCK
    past = n_iota < tok_blk
    zero_rows = jnp.zeros((AUG - HEAD_DIM - MASK_SLOTS, tm), F32)
    ones_row = jnp.where(lax.broadcasted_iota(jnp.int32, (V_ROWS - HEAD_DIM, tm), 0) == 0, 1.0, 0.0)
    for hd in range(N_HEADS):
        qth = qt[hd * HEAD_DIM:(hd + 1) * HEAD_DIM]
        gate = jnp.dot(km[:, hd * HEAD_DIM:(hd + 1) * HEAD_DIM].astype(BF16), qth.astype(BF16),
                       preferred_element_type=F32)
        g = jnp.where(past, gate, NEG)
        sel = n_iota == tok_blk
        for _ in range(MOBA_TOPK):
            m = jnp.max(g, axis=0, keepdims=True)
            idx = jnp.min(jnp.where(g == m, n_iota, MASK_SLOTS), axis=0, keepdims=True)
            pick = n_iota == idx
            sel = sel | (pick & past)
            g = jnp.where(pick, -jnp.inf, g)
        qt_ref[0, hd, 0] = jnp.concatenate(
            [qth * (HEAD_DIM ** -0.5 * LOG2E), jnp.where(sel, 0.0, NEG), zero_rows], axis=0).astype(BF16)
        vt_ref[0, hd, 0] = jnp.concatenate(
            [vt[hd * HEAD_DIM:(hd + 1) * HEAD_DIM], ones_row], axis=0).astype(BF16)

    lane = lax.broadcasted_iota(jnp.int32, (tm, LANES), 1)
    row_blk = blk0 + lax.broadcasted_iota(jnp.int32, (tm, LANES), 0) // MOBA_BLOCK
    k_pat = jnp.where(lane - HEAD_DIM == row_blk, 1.0, 0.0)
    low = lane < HEAD_DIM
    for hp in range(N_HEADS // 2):
        kt = k[:, hp * LANES:(hp + 1) * LANES]
        k_ref[0, 2 * hp] = jnp.where(low, kt, k_pat).astype(BF16)
        k_ref[0, 2 * hp + 1] = jnp.where(low, pltpu.roll(kt, HEAD_DIM, 1), k_pat).astype(BF16)


def _projection(x, g, w_in, conv_w, w_conv_out):
    b, s, d = x.shape
    tm = TOKEN_TILE
    const = lambda shape: pl.BlockSpec(shape, lambda bi, si: (0,) * len(shape),
                                       pipeline_mode=pl.Buffered(1))
    tok = pl.BlockSpec((1, tm, d), lambda bi, si: (bi, si, 0))
    return pl.pallas_call(
        _proj_kernel,
        grid=(b, s // tm),
        in_specs=[tok, const((1, d)), const((d, N_IN)), const((CONV_K, CONV_WIDTH)),
                  const((CONV_WIDTH, d))],
        out_specs=[pl.BlockSpec((1, N_HEADS, 1, AUG, tm), lambda bi, si: (bi, 0, si, 0, 0)),
                   pl.BlockSpec((1, N_HEADS, tm, AUG), lambda bi, si: (bi, 0, si, 0)),
                   pl.BlockSpec((1, N_HEADS, 1, V_ROWS, tm), lambda bi, si: (bi, 0, si, 0, 0)),
                   tok, tok],
        out_shape=[jax.ShapeDtypeStruct((b, N_HEADS, s // tm, AUG, tm), BF16),
                   jax.ShapeDtypeStruct((b, N_HEADS, s, AUG), BF16),
                   jax.ShapeDtypeStruct((b, N_HEADS, s // tm, V_ROWS, tm), BF16),
                   jax.ShapeDtypeStruct((b, s, d), F32),
                   jax.ShapeDtypeStruct((b, s, d), F32)],
        scratch_shapes=[pltpu.VMEM((8, CONV_WIDTH), F32),
                        pltpu.VMEM((MASK_SLOTS, ATTN_WIDTH), F32)],
        compiler_params=pltpu.CompilerParams(
            dimension_semantics=("arbitrary", "arbitrary"), vmem_limit_bytes=VMEM_LIMIT),
        name="moba_projection",
    )(x, g, w_in, conv_w, w_conv_out)


def _attn_kernel(qa_ref, qb_ref, k_ref, vt_ref, brow_ref, oa_ref, ob_ref,
                 q_ref, s_ref, tmax_ref, m_ref, acc_ref, bias_ref, *, n_tiles):
    pr = pl.program_id(2)
    n_items = n_tiles + 1
    n_far = n_items - 2 * N_NEAR
    n_heads = qa_ref.shape[1]
    tile_a, tile_b = pr, n_tiles - 1 - pr

    zero_tile = n_heads * N_NEAR

    @pl.when((pl.program_id(1) == 0) & (pr == 0))
    def _():
        kb = lax.broadcasted_iota(jnp.int32, (SUPER, SUPER), 0) // MOBA_BLOCK
        qb = lax.broadcasted_iota(jnp.int32, (SUPER, SUPER), 1) // MOBA_BLOCK
        for hh in range(n_heads):
            for delta in range(N_NEAR):
                rows = jnp.broadcast_to(brow_ref[hh, delta], (SUPER, 2 * SUPER))
                tile = pltpu.roll(rows, 0, 1, stride=1, stride_axis=0)[:, :SUPER]
                dist = (SUPER // MOBA_BLOCK) * delta + qb - kb
                bias_ref[hh * N_NEAR + delta] = jnp.where(dist >= FAR_DIST, 0.0, tile)
        bias_ref[zero_tile] = jnp.zeros((SUPER, SUPER), F32)

    def item(w):
        far_b = jnp.minimum(tile_b + 1 - N_NEAR, n_far)
        rest_b = tile_b + 1 - far_b
        u = w - n_far
        is_far = w < n_far
        second = jnp.where(is_far, w < far_b, u < rest_b)
        t = jnp.where(second, jnp.where(is_far, w, far_b + u),
                      jnp.where(is_far, w - far_b, n_far - far_b + u - rest_b))
        return second.astype(jnp.int32), jnp.where(second, tile_b, tile_a), t

    def produce(w, slot, near, heads=None):
        second, qi, t = item(w)
        keys = pl.ds(pl.multiple_of(t * SUPER, SUPER), SUPER)
        for hh in (range(n_heads) if heads is None else heads):
            st = jnp.dot(k_ref[0, hh, keys, :], q_ref[second, hh], preferred_element_type=F32)
            if near:
                delta = qi - t
                st = st + bias_ref[jnp.where(delta < N_NEAR, hh * N_NEAR + delta, zero_tile)]
            s_ref[slot, hh] = st
            tmax_ref[slot, hh] = jnp.max(st, axis=0, keepdims=True)

    def consume(w, slot, heads=None):
        second, _, t = item(w)
        for hh in (range(n_heads) if heads is None else heads):
            m = m_ref[second, hh]
            m_new = jnp.maximum(m, tmax_ref[slot, hh])
            p = jnp.exp2(s_ref[slot, hh] - m_new).astype(BF16)
            pv = jnp.dot(vt_ref[0, hh, t], p, preferred_element_type=F32)
            acc_ref[second, hh] = jnp.exp2(m - m_new) * acc_ref[second, hh] + pv
            m_ref[second, hh] = m_new

    q_ref[0] = qa_ref[0, :, 0]
    q_ref[1] = qb_ref[0, :, 0]
    m_ref[...] = jnp.full(m_ref.shape, -jnp.inf, F32)
    acc_ref[...] = jnp.zeros(acc_ref.shape, F32)
    produce(0, 0, near=False)

    def pair(jj, _, near):
        for hh in range(n_heads):
            produce(2 * jj + 1, 1, near, [hh])
            consume(2 * jj, 0, [hh])
        for hh in range(n_heads):
            produce(2 * jj + 2, 0, near, [hh])
            consume(2 * jj + 1, 1, [hh])
        return _

    far_pairs = (n_far - 1) // 2
    lax.fori_loop(0, far_pairs, functools.partial(pair, near=False), 0, unroll=2)
    lax.fori_loop(far_pairs, n_items // 2, functools.partial(pair, near=True), 0, unroll=2)
    consume(n_items - 1, 0)

    for second, o_ref in enumerate((oa_ref, ob_ref)):
        outs = [(acc_ref[second, hh, :HEAD_DIM] / acc_ref[second, hh, HEAD_DIM:HEAD_DIM + 1]).T
                for hh in range(n_heads)]
        o_ref[0, 0, 0] = jnp.concatenate(outs, axis=-1).astype(o_ref.dtype)


def _attention(qt, k_aug, vt, bias_rows):
    b, _, n_tiles, _, _ = qt.shape
    s = n_tiles * SUPER
    hb = 4
    half = n_tiles // 2
    q_spec = lambda tile: pl.BlockSpec((1, hb, 1, AUG, SUPER), lambda hp, bi, pr: (bi, hp, tile(pr), 0, 0))
    o_spec = lambda tile: pl.BlockSpec((1, 1, 1, SUPER, hb * HEAD_DIM),
                                       lambda hp, bi, pr: (bi, tile(pr), hp, 0, 0))
    o_shape = jax.ShapeDtypeStruct((b, half, N_HEADS // hb, SUPER, hb * HEAD_DIM), BF16)
    return pl.pallas_call(
        functools.partial(_attn_kernel, n_tiles=n_tiles),
        grid=(N_HEADS // hb, b, half),
        in_specs=[q_spec(lambda pr: pr), q_spec(lambda pr: n_tiles - 1 - pr),
                  pl.BlockSpec((1, hb, s, AUG), lambda hp, bi, pr: (bi, hp, 0, 0)),
                  pl.BlockSpec((1, hb, n_tiles, V_ROWS, SUPER), lambda hp, bi, pr: (bi, hp, 0, 0, 0)),
                  pl.BlockSpec((hb, N_NEAR, 1, 2 * SUPER), lambda hp, bi, pr: (hp, 0, 0, 0))],
        out_specs=[o_spec(lambda pr: pr), o_spec(lambda pr: half - 1 - pr)],
        out_shape=[o_shape, o_shape],
        scratch_shapes=[pltpu.VMEM((2, hb, AUG, SUPER), BF16),
                        pltpu.VMEM((2, hb, SUPER, SUPER), F32),
                        pltpu.VMEM((2, hb, 1, SUPER), F32),
                        pltpu.VMEM((2, hb, 1, SUPER), F32),
                        pltpu.VMEM((2, hb, V_ROWS, SUPER), F32),
                        pltpu.VMEM((hb * N_NEAR + 1, SUPER, SUPER), F32)],
        compiler_params=pltpu.CompilerParams(
            dimension_semantics=("arbitrary", "arbitrary", "arbitrary"),
            vmem_limit_bytes=VMEM_LIMIT),
        name="moba_attention",
    )(qt, qt, k_aug, vt, bias_rows)


def _out_kernel(x_ref, att_lo_ref, att_hi_ref, gc_ref, sga_ref, wao_ref, wo_ref, gpost_ref, gpre_ref,
                w1_ref, w2_ref, gmpost_ref, o_ref):
    first_half = pl.program_id(1) < pl.num_programs(1) // 2
    att = jnp.where(first_half, att_lo_ref[0, 0], att_hi_ref[0, 0])
    att = jnp.concatenate([att[i] for i in range(att.shape[0])], axis=-1)
    ya = jnp.dot(att, wao_ref[...], preferred_element_type=F32)
    m = (gc_ref[0] + sga_ref[0] * ya).astype(BF16)
    mix = jnp.dot(m, wo_ref[...], preferred_element_type=F32)
    x1 = x_ref[0] + _rms(mix, gpost_ref[...])
    h2 = _rms(x1, gpre_ref[...]).astype(BF16)
    f = jnp.zeros(x1.shape, F32)
    for ci in range(D_FF // FF_CHUNK):
        cols = slice(ci * FF_CHUNK, (ci + 1) * FF_CHUNK)
        act = jnp.dot(h2, w1_ref[:, cols], preferred_element_type=F32)
        act = jnp.square(jnp.maximum(act, 0.0)).astype(BF16)
        f = f + jnp.dot(act, w2_ref[cols, :], preferred_element_type=F32)
    o_ref[0] = x1 + _rms(f, gmpost_ref[...])


def _output(x, att_lo, att_hi, gc, sga, w_attn_out, w_o, g_post, g_mlp_pre, w1, w2, g_mlp_post):
    b, s, d = x.shape
    tm = TOKEN_TILE
    half = s // tm // 2
    const = lambda shape: pl.BlockSpec(shape, lambda bi, si: (0,) * len(shape),
                                       pipeline_mode=pl.Buffered(1))
    tok = pl.BlockSpec((1, tm, d), lambda bi, si: (bi, si, 0))
    att_spec = lambda tile: pl.BlockSpec((1, 1) + att_lo.shape[2:], lambda bi, si: (bi, tile(si), 0, 0, 0))
    return pl.pallas_call(
        _out_kernel,
        grid=(b, s // tm),
        in_specs=[tok,
                  att_spec(lambda si: jnp.minimum(si, half - 1)),
                  att_spec(lambda si: jnp.maximum(si - half, 0)),
                  tok, tok,
                  const((ATTN_WIDTH, d)), const((d, d)), const((1, d)), const((1, d)),
                  const((d, D_FF)), const((D_FF, d)), const((1, d))],
        out_specs=tok,
        out_shape=jax.ShapeDtypeStruct((b, s, d), F32),
        compiler_params=pltpu.CompilerParams(
            dimension_semantics=("arbitrary", "arbitrary"), vmem_limit_bytes=VMEM_LIMIT),
        name="moba_output_mlp",
    )(x, att_lo, att_hi, gc, sga, w_attn_out, w_o, g_post, g_mlp_pre, w1, w2, g_mlp_post)


def kernel(x, ln_mix_pre, w_in, conv_w, w_conv_out, w_attn_out, rel_bias, w_o, ln_mix_post,
           ln_mlp_pre, w_mlp_in, w_mlp_out, ln_mlp_post):
    depth = w_in.shape[0]
    b, s, d = x.shape
    assert d == D_MODEL and TOKEN_TILE == SUPER and s % (2 * SUPER) == 0 and s // SUPER >= 2 * N_NEAR
    assert MOBA_TOPK <= s // MOBA_BLOCK <= MASK_SLOTS
    bias_rows = _bias_rows(rel_bias)
    for l in range(depth):
        qt, k_aug, vt, gc, sga = _projection(
            x, ln_mix_pre[l][None], w_in[l].astype(BF16), conv_w[l], w_conv_out[l].astype(BF16))
        att_lo, att_hi = _attention(qt, k_aug, vt, bias_rows)
        x = _output(x, att_lo, att_hi, gc, sga, w_attn_out[l].astype(BF16), w_o[l].astype(BF16),
                    ln_mix_post[l][None], ln_mlp_pre[l][None], w_mlp_in[l].astype(BF16),
                    w_mlp_out[l].astype(BF16), ln_mlp_post[l][None])
    return x
```

```python
import functools
import math

import jax
import jax.numpy as jnp
from jax import lax
from jax.experimental import pallas as pl
from jax.experimental.pallas import tpu as pltpu

D_MODEL = 1024
CONV_WIDTH = D_MODEL // 2
CONV_K = 3
N_HEADS = 8
HEAD_DIM = 64
ATTN_WIDTH = N_HEADS * HEAD_DIM
MOBA_BLOCK = 256
MOBA_TOPK = 3
N_BUCKETS = 32
MAX_DISTANCE = 1024
D_FF = 4 * D_MODEL
RMS_EPS = 1e-6
N_IN = 3 * CONV_WIDTH + 3 * ATTN_WIDTH + 2 * D_MODEL
NEG = -1e30
LOG2E = math.log2(math.e)

LANES = 128
BF16_SUBLANES = 16
AUG = 2 * HEAD_DIM
MASK_SLOTS = (AUG - HEAD_DIM) // 2
V_ROWS = HEAD_DIM + BF16_SUBLANES
SUPER = 2 * MOBA_BLOCK
FAR_DIST = -(-(MAX_DISTANCE + MOBA_BLOCK) // MOBA_BLOCK)
N_NEAR = (FAR_DIST + 1) // 2
TOKEN_TILE = 512
FF_CHUNK = 1024
ROW_GROUPS = 2
VMEM_LIMIT = 56 * 1024 * 1024

F32 = jnp.float32
BF16 = jnp.bfloat16


def _rms(x, g):
    return (x * lax.rsqrt(jnp.mean(x * x, axis=-1, keepdims=True) + RMS_EPS)) * g


def _t5_causal_bucket(rel):
    n = jnp.maximum(rel, 0)
    max_exact = N_BUCKETS // 2
    nf = jnp.maximum(n, max_exact).astype(F32)
    large = max_exact + (jnp.log(nf / max_exact) / math.log(MAX_DISTANCE / max_exact)
                         * (N_BUCKETS - max_exact)).astype(jnp.int32)
    large = jnp.minimum(large, N_BUCKETS - 1)
    return jnp.where(n < max_exact, n, large)


def _bias_rows(rel_bias):
    m = jnp.arange(2 * SUPER)
    delta = jnp.arange(N_NEAR)[:, None]
    rel = delta * SUPER + jnp.where(m < SUPER, m, m - 2 * SUPER)
    far = rel_bias[:, _t5_causal_bucket(jnp.int32(FAR_DIST * MOBA_BLOCK))]
    w = jnp.where(rel >= 0, (rel_bias[:, _t5_causal_bucket(rel)] - far[:, None, None]) * LOG2E, NEG)
    return w.astype(F32)[:, :, None, :]


def _proj_kernel(x_ref, g_ref, win_ref, cw_ref, wco_ref,
                 qt_ref, k_ref, vt_ref, gc_ref, sga_ref,
                 carry_ref, kmean_ref):
    tm = x_ref.shape[1]
    c, a, d = CONV_WIDTH, ATTN_WIDTH, D_MODEL
    s = pl.program_id(1)
    blocks_per_tile = tm // MOBA_BLOCK
    blk0 = s * blocks_per_tile

    @pl.when(s == 0)
    def _():
        carry_ref[...] = jnp.zeros_like(carry_ref)
        kmean_ref[...] = jnp.zeros_like(kmean_ref)

    h = _rms(x_ref[0], g_ref[...]).astype(BF16)

    def seg(lo, width):
        return jnp.dot(h, win_ref[:, lo:lo + width], preferred_element_type=F32)

    u = seg(2 * c, c) * seg(0, c)
    row = lax.broadcasted_iota(jnp.int32, u.shape, 0)
    prev = carry_ref[...]
    p1, p2 = prev[7:8], prev[6:7]
    u1 = jnp.where(row == 0, p1, pltpu.roll(u, 1, 0))
    u2 = jnp.where(row == 0, p2, jnp.where(row == 1, p1, pltpu.roll(u, 2, 0)))
    carry_ref[...] = u[tm - 8:]
    cw = cw_ref[...]
    conv = cw[2:3] * u + cw[1:2] * u1 + cw[0:1] * u2
    z = (seg(c, c) * conv).astype(BF16)
    yc = jnp.dot(z, wco_ref[...], preferred_element_type=F32)
    gc_ref[0] = jax.nn.sigmoid(seg(3 * c + 3 * a, d)) * yc
    sga_ref[0] = jax.nn.sigmoid(seg(3 * c + 3 * a + d, d))

    q = seg(3 * c, a)
    k = seg(3 * c + a, a)
    v = seg(3 * c + 2 * a, a)

    rows_nb = lax.broadcasted_iota(jnp.int32, kmean_ref.shape, 0)
    km = kmean_ref[...]
    for bi in range(blocks_per_tile):
        ks = jnp.sum(k[bi * MOBA_BLOCK:(bi + 1) * MOBA_BLOCK], axis=0, keepdims=True) * (1.0 / MOBA_BLOCK)
        km = jnp.where(rows_nb == blk0 + bi, ks, km)
    kmean_ref[...] = km

    qt = q.T
    vt = v.T
    n_iota = lax.broadcasted_iota(jnp.int32, (MASK_SLOTS, tm), 0)
    tok_blk = blk0 + lax.broadcasted_iota(jnp.int32, (MASK_SLOTS, tm), 1) // MOBA_BLOCK
    past = n_iota < tok_blk
    zero_rows = jnp.zeros((AUG - HEAD_DIM - MASK_SLOTS, tm), F32)
    ones_row = jnp.where(lax.broadcasted_iota(jnp.int32, (V_ROWS - HEAD_DIM, tm), 0) == 0, 1.0, 0.0)
    for hd in range(N_HEADS):
        qth = qt[hd * HEAD_DIM:(hd + 1) * HEAD_DIM]
        gate = jnp.dot(km[:, hd * HEAD_DIM:(hd + 1) * HEAD_DIM].astype(BF16), qth.astype(BF16),
                       preferred_element_type=F32)
        g = jnp.where(past, gate, NEG)
        sel = n_iota == tok_blk
        for _ in range(MOBA_TOPK):
            m = jnp.max(g, axis=0, keepdims=True)
            idx = jnp.min(jnp.where(g == m, n_iota, MASK_SLOTS), axis=0, keepdims=True)
            pick = n_iota == idx
            sel = sel | (pick & past)
            g = jnp.where(pick, -jnp.inf, g)
        qt_ref[0, hd, 0] = jnp.concatenate(
            [qth * (HEAD_DIM ** -0.5 * LOG2E), jnp.where(sel, 0.0, NEG), zero_rows], axis=0).astype(BF16)
        vt_ref[0, hd, 0] = jnp.concatenate(
            [vt[hd * HEAD_DIM:(hd + 1) * HEAD_DIM], ones_row], axis=0).astype(BF16)

    lane = lax.broadcasted_iota(jnp.int32, (tm, LANES), 1)
    row_blk = blk0 + lax.broadcasted_iota(jnp.int32, (tm, LANES), 0) // MOBA_BLOCK
    k_pat = jnp.where(lane - HEAD_DIM == row_blk, 1.0, 0.0)
    low = lane < HEAD_DIM
    for hp in range(N_HEADS // 2):
        kt = k[:, hp * LANES:(hp + 1) * LANES]
        k_ref[0, 2 * hp] = jnp.where(low, kt, k_pat).astype(BF16)
        k_ref[0, 2 * hp + 1] = jnp.where(low, pltpu.roll(kt, HEAD_DIM, 1), k_pat).astype(BF16)


def _projection(x, g, w_in, conv_w, w_conv_out):
    b, s, d = x.shape
    tm = TOKEN_TILE
    const = lambda shape: pl.BlockSpec(shape, lambda bi, si: (0,) * len(shape),
                                       pipeline_mode=pl.Buffered(1))
    tok = pl.BlockSpec((1, tm, d), lambda bi, si: (bi, si, 0))
    return pl.pallas_call(
        _proj_kernel,
        grid=(b, s // tm),
        in_specs=[tok, const((1, d)), const((d, N_IN)), const((CONV_K, CONV_WIDTH)),
                  const((CONV_WIDTH, d))],
        out_specs=[pl.BlockSpec((1, N_HEADS, 1, AUG, tm), lambda bi, si: (bi, 0, si, 0, 0)),
                   pl.BlockSpec((1, N_HEADS, tm, AUG), lambda bi, si: (bi, 0, si, 0)),
                   pl.BlockSpec((1, N_HEADS, 1, V_ROWS, tm), lambda bi, si: (bi, 0, si, 0, 0)),
                   tok, tok],
        out_shape=[jax.ShapeDtypeStruct((b, N_HEADS, s // tm, AUG, tm), BF16),
                   jax.ShapeDtypeStruct((b, N_HEADS, s, AUG), BF16),
                   jax.ShapeDtypeStruct((b, N_HEADS, s // tm, V_ROWS, tm), BF16),
                   jax.ShapeDtypeStruct((b, s, d), F32),
                   jax.ShapeDtypeStruct((b, s, d), F32)],
        scratch_shapes=[pltpu.VMEM((8, CONV_WIDTH), F32),
                        pltpu.VMEM((MASK_SLOTS, ATTN_WIDTH), F32)],
        compiler_params=pltpu.CompilerParams(
            dimension_semantics=("arbitrary", "arbitrary"), vmem_limit_bytes=VMEM_LIMIT),
        name="moba_projection",
    )(x, g, w_in, conv_w, w_conv_out)


def _attn_kernel(qa_ref, qb_ref, k_ref, vt_ref, brow_ref, oa_ref, ob_ref,
                 q_ref, s_ref, tmax_ref, m_ref, acc_ref, bias_ref, *, n_tiles):
    pr = pl.program_id(2)
    n_items = n_tiles + 1
    n_far = n_items - 2 * N_NEAR
    n_heads = qa_ref.shape[1]
    tile_a, tile_b = pr, n_tiles - 1 - pr

    zero_tile = n_heads * N_NEAR

    @pl.when((pl.program_id(1) == 0) & (pr == 0))
    def _():
        kb = lax.broadcasted_iota(jnp.int32, (SUPER, SUPER), 0) // MOBA_BLOCK
        qb = lax.broadcasted_iota(jnp.int32, (SUPER, SUPER), 1) // MOBA_BLOCK
        for hh in range(n_heads):
            for delta in range(N_NEAR):
                rows = jnp.broadcast_to(brow_ref[hh, delta], (SUPER, 2 * SUPER))
                tile = pltpu.roll(rows, 0, 1, stride=1, stride_axis=0)[:, :SUPER]
                dist = (SUPER // MOBA_BLOCK) * delta + qb - kb
                bias_ref[hh * N_NEAR + delta] = jnp.where(dist >= FAR_DIST, 0.0, tile)
        bias_ref[zero_tile] = jnp.zeros((SUPER, SUPER), F32)

    def item(w):
        far_b = jnp.minimum(tile_b + 1 - N_NEAR, n_far)
        rest_b = tile_b + 1 - far_b
        u = w - n_far
        is_far = w < n_far
        second = jnp.where(is_far, w < far_b, u < rest_b)
        t = jnp.where(second, jnp.where(is_far, w, far_b + u),
                      jnp.where(is_far, w - far_b, n_far - far_b + u - rest_b))
        return second.astype(jnp.int32), jnp.where(second, tile_b, tile_a), t

    def produce(w, slot, near, heads=None):
        second, qi, t = item(w)
        keys = pl.ds(pl.multiple_of(t * SUPER, SUPER), SUPER)
        for hh in (range(n_heads) if heads is None else heads):
            st = jnp.dot(k_ref[0, hh, keys, :], q_ref[second, hh], preferred_element_type=F32)
            if near:
                delta = qi - t
                st = st + bias_ref[jnp.where(delta < N_NEAR, hh * N_NEAR + delta, zero_tile)]
            s_ref[slot, hh] = st
            tmax_ref[slot, hh] = jnp.max(st, axis=0, keepdims=True)

    def consume(w, slot, heads=None):
        second, _, t = item(w)
        for hh in (range(n_heads) if heads is None else heads):
            m = m_ref[second, hh]
            m_new = jnp.maximum(m, tmax_ref[slot, hh])
            p = jnp.exp2(s_ref[slot, hh] - m_new).astype(BF16)
            pv = jnp.dot(vt_ref[0, hh, t], p, preferred_element_type=F32)
            acc_ref[second, hh] = jnp.exp2(m - m_new) * acc_ref[second, hh] + pv
            m_ref[second, hh] = m_new

    q_ref[0] = qa_ref[0, :, 0]
    q_ref[1] = qb_ref[0, :, 0]
    m_ref[...] = jnp.full(m_ref.shape, -jnp.inf, F32)
    acc_ref[...] = jnp.zeros(acc_ref.shape, F32)
    produce(0, 0, near=False)

    def pair(jj, _, near):
        for hh in range(n_heads):
            produce(2 * jj + 1, 1, near, [hh])
            consume(2 * jj, 0, [hh])
        for hh in range(n_heads):
            produce(2 * jj + 2, 0, near, [hh])
            consume(2 * jj + 1, 1, [hh])
        return _

    far_pairs = (n_far - 1) // 2
    lax.fori_loop(0, far_pairs, functools.partial(pair, near=False), 0, unroll=2)
    lax.fori_loop(far_pairs, n_items // 2, functools.partial(pair, near=True), 0, unroll=2)
    consume(n_items - 1, 0)

    for second, o_ref in enumerate((oa_ref, ob_ref)):
        outs = [(acc_ref[second, hh, :HEAD_DIM] / acc_ref[second, hh, HEAD_DIM:HEAD_DIM + 1]).T
                for hh in range(n_heads)]
        o_ref[0, 0, 0] = jnp.concatenate(outs, axis=-1).astype(o_ref.dtype)


def _attention(qt, k_aug, vt, bias_rows):
    b, _, n_tiles, _, _ = qt.shape
    s = n_tiles * SUPER
    hb = 4
    half = n_tiles // 2
    q_spec = lambda tile: pl.BlockSpec((1, hb, 1, AUG, SUPER), lambda hp, bi, pr: (bi, hp, tile(pr), 0, 0))
    o_spec = lambda tile: pl.BlockSpec((1, 1, 1, SUPER, hb * HEAD_DIM),
                                       lambda hp, bi, pr: (bi, tile(pr), hp, 0, 0))
    o_shape = jax.ShapeDtypeStruct((b, half, N_HEADS // hb, SUPER, hb * HEAD_DIM), BF16)
    return pl.pallas_call(
        functools.partial(_attn_kernel, n_tiles=n_tiles),
        grid=(N_HEADS // hb, b, half),
        in_specs=[q_spec(lambda pr: pr), q_spec(lambda pr: n_tiles - 1 - pr),
                  pl.BlockSpec((1, hb, s, AUG), lambda hp, bi, pr: (bi, hp, 0, 0)),
                  pl.BlockSpec((1, hb, n_tiles, V_ROWS, SUPER), lambda hp, bi, pr: (bi, hp, 0, 0, 0)),
                  pl.BlockSpec((hb, N_NEAR, 1, 2 * SUPER), lambda hp, bi, pr: (hp, 0, 0, 0))],
        out_specs=[o_spec(lambda pr: pr), o_spec(lambda pr: half - 1 - pr)],
        out_shape=[o_shape, o_shape],
        scratch_shapes=[pltpu.VMEM((2, hb, AUG, SUPER), BF16),
                        pltpu.VMEM((2, hb, SUPER, SUPER), F32),
                        pltpu.VMEM((2, hb, 1, SUPER), F32),
                        pltpu.VMEM((2, hb, 1, SUPER), F32),
                        pltpu.VMEM((2, hb, V_ROWS, SUPER), F32),
                        pltpu.VMEM((hb * N_NEAR + 1, SUPER, SUPER), F32)],
        compiler_params=pltpu.CompilerParams(
            dimension_semantics=("arbitrary", "arbitrary", "arbitrary"),
            vmem_limit_bytes=VMEM_LIMIT),
        name="moba_attention",
    )(qt, qt, k_aug, vt, bias_rows)


def _out_kernel(x_ref, att_lo_ref, att_hi_ref, gc_ref, sga_ref, wao_ref, wo_ref, gpost_ref, gpre_ref,
                w1_ref, w2_ref, gmpost_ref, o_ref):
    first_half = pl.program_id(1) < pl.num_programs(1) // 2
    att = jnp.where(first_half, att_lo_ref[0, 0], att_hi_ref[0, 0])
    att = jnp.concatenate([att[i] for i in range(att.shape[0])], axis=-1)
    tm = att.shape[0]
    rows = [slice(i * tm // ROW_GROUPS, (i + 1) * tm // ROW_GROUPS) for i in range(ROW_GROUPS)]
    ya = [jnp.dot(att[r], wao_ref[...], preferred_element_type=F32) for r in rows]
    m = [(gc_ref[0, r] + sga_ref[0, r] * y).astype(BF16) for r, y in zip(rows, ya)]
    mix = [jnp.dot(mi, wo_ref[...], preferred_element_type=F32) for mi in m]
    x1 = [x_ref[0, r] + _rms(mx, gpost_ref[...]) for r, mx in zip(rows, mix)]
    h2 = [_rms(xi, gpre_ref[...]).astype(BF16) for xi in x1]
    f = [jnp.zeros(xi.shape, F32) for xi in x1]
    for ci in range(D_FF // FF_CHUNK):
        cols = slice(ci * FF_CHUNK, (ci + 1) * FF_CHUNK)
        for i in range(ROW_GROUPS):
            act = jnp.dot(h2[i], w1_ref[:, cols], preferred_element_type=F32)
            act = jnp.square(jnp.maximum(act, 0.0)).astype(BF16)
            f[i] = f[i] + jnp.dot(act, w2_ref[cols, :], preferred_element_type=F32)
    for r, xi, fi in zip(rows, x1, f):
        o_ref[0, r] = xi + _rms(fi, gmpost_ref[...])


def _output(x, att_lo, att_hi, gc, sga, w_attn_out, w_o, g_post, g_mlp_pre, w1, w2, g_mlp_post):
    b, s, d = x.shape
    tm = TOKEN_TILE
    half = s // tm // 2
    const = lambda shape: pl.BlockSpec(shape, lambda bi, si: (0,) * len(shape),
                                       pipeline_mode=pl.Buffered(1))
    tok = pl.BlockSpec((1, tm, d), lambda bi, si: (bi, si, 0))
    att_spec = lambda tile: pl.BlockSpec((1, 1) + att_lo.shape[2:], lambda bi, si: (bi, tile(si), 0, 0, 0))
    return pl.pallas_call(
        _out_kernel,
        grid=(b, s // tm),
        in_specs=[tok,
                  att_spec(lambda si: jnp.minimum(si, half - 1)),
                  att_spec(lambda si: jnp.maximum(si - half, 0)),
                  tok, tok,
                  const((ATTN_WIDTH, d)), const((d, d)), const((1, d)), const((1, d)),
                  const((d, D_FF)), const((D_FF, d)), const((1, d))],
        out_specs=tok,
        out_shape=jax.ShapeDtypeStruct((b, s, d), F32),
        compiler_params=pltpu.CompilerParams(
            dimension_semantics=("arbitrary", "arbitrary"), vmem_limit_bytes=VMEM_LIMIT),
        name="moba_output_mlp",
    )(x, att_lo, att_hi, gc, sga, w_attn_out, w_o, g_post, g_mlp_pre, w1, w2, g_mlp_post)


def kernel(x, ln_mix_pre, w_in, conv_w, w_conv_out, w_attn_out, rel_bias, w_o, ln_mix_post,
           ln_mlp_pre, w_mlp_in, w_mlp_out, ln_mlp_post):
    depth = w_in.shape[0]
    b, s, d = x.shape
    assert d == D_MODEL and TOKEN_TILE == SUPER and s % (2 * SUPER) == 0 and s // SUPER >= 2 * N_NEAR
    assert MOBA_TOPK <= s // MOBA_BLOCK <= MASK_SLOTS
    bias_rows = _bias_rows(rel_bias)
    for l in range(depth):
        qt, k_aug, vt, gc, sga = _projection(
            x, ln_mix_pre[l][None], w_in[l].astype(BF16), conv_w[l], w_conv_out[l].astype(BF16))
        att_lo, att_hi = _attention(qt, k_aug, vt, bias_rows)
        x = _output(x, att_lo, att_hi, gc, sga, w_attn_out[l].astype(BF16), w_o[l].astype(BF16),
                    ln_mix_post[l][None], ln_mlp_pre[l][None], w_mlp_in[l].astype(BF16),
                    w_mlp_out[l].astype(BF16), ln_mlp_post[l][None])
    return x
```

```python
import functools
import math

import jax
import jax.numpy as jnp
from jax import lax
from jax.experimental import pallas as pl
from jax.experimental.pallas import tpu as pltpu

D_MODEL = 1024
CONV_WIDTH = D_MODEL // 2
CONV_K = 3
N_HEADS = 8
HEAD_DIM = 64
ATTN_WIDTH = N_HEADS * HEAD_DIM
MOBA_BLOCK = 256
MOBA_TOPK = 3
N_BUCKETS = 32
MAX_DISTANCE = 1024
D_FF = 4 * D_MODEL
RMS_EPS = 1e-6
N_IN = 3 * CONV_WIDTH + 3 * ATTN_WIDTH + 2 * D_MODEL
NEG = -1e30
LOG2E = math.log2(math.e)

LANES = 128
BF16_SUBLANES = 16
AUG = 2 * HEAD_DIM
MASK_SLOTS = (AUG - HEAD_DIM) // 2
V_ROWS = HEAD_DIM + BF16_SUBLANES
SUPER = 2 * MOBA_BLOCK
FAR_DIST = -(-(MAX_DISTANCE + MOBA_BLOCK) // MOBA_BLOCK)
N_NEAR = (FAR_DIST + 1) // 2
TOKEN_TILE = 512
FF_CHUNK = 1024
ROW_GROUPS = 2
PROJ_GROUPS = 2
VMEM_LIMIT = 56 * 1024 * 1024

F32 = jnp.float32
BF16 = jnp.bfloat16


def _rms(x, g):
    return (x * lax.rsqrt(jnp.mean(x * x, axis=-1, keepdims=True) + RMS_EPS)) * g


def _t5_causal_bucket(rel):
    n = jnp.maximum(rel, 0)
    max_exact = N_BUCKETS // 2
    nf = jnp.maximum(n, max_exact).astype(F32)
    large = max_exact + (jnp.log(nf / max_exact) / math.log(MAX_DISTANCE / max_exact)
                         * (N_BUCKETS - max_exact)).astype(jnp.int32)
    large = jnp.minimum(large, N_BUCKETS - 1)
    return jnp.where(n < max_exact, n, large)


def _bias_rows(rel_bias):
    m = jnp.arange(2 * SUPER)
    delta = jnp.arange(N_NEAR)[:, None]
    rel = delta * SUPER + jnp.where(m < SUPER, m, m - 2 * SUPER)
    far = rel_bias[:, _t5_causal_bucket(jnp.int32(FAR_DIST * MOBA_BLOCK))]
    w = jnp.where(rel >= 0, (rel_bias[:, _t5_causal_bucket(rel)] - far[:, None, None]) * LOG2E, NEG)
    return w.astype(F32)[:, :, None, :]


def _proj_kernel(x_ref, g_ref, win_ref, cw_ref, wco_ref,
                 qt_ref, k_ref, vt_ref, gc_ref, sga_ref,
                 carry_ref, kmean_ref):
    tm = x_ref.shape[1]
    tg = tm // PROJ_GROUPS
    c, a, d = CONV_WIDTH, ATTN_WIDTH, D_MODEL
    s = pl.program_id(1)
    assert tg % MOBA_BLOCK == 0
    blocks_per_group = tg // MOBA_BLOCK

    @pl.when(s == 0)
    def _():
        carry_ref[...] = jnp.zeros_like(carry_ref)
        kmean_ref[...] = jnp.zeros_like(kmean_ref)

    cw = cw_ref[...]
    rows_nb = lax.broadcasted_iota(jnp.int32, kmean_ref.shape, 0)
    row = lax.broadcasted_iota(jnp.int32, (tg, c), 0)
    n_iota = lax.broadcasted_iota(jnp.int32, (MASK_SLOTS, tg), 0)
    zero_rows = jnp.zeros((AUG - HEAD_DIM - MASK_SLOTS, tg), F32)
    ones_row = jnp.where(lax.broadcasted_iota(jnp.int32, (V_ROWS - HEAD_DIM, tg), 0) == 0, 1.0, 0.0)
    lane = lax.broadcasted_iota(jnp.int32, (tg, LANES), 1)
    low = lane < HEAD_DIM

    km = kmean_ref[...]
    prev = carry_ref[...]
    for gi in range(PROJ_GROUPS):
        rows = slice(gi * tg, (gi + 1) * tg)
        blk0 = (s * PROJ_GROUPS + gi) * blocks_per_group
        h = _rms(x_ref[0, rows], g_ref[...]).astype(BF16)

        def seg(lo, width, h=h):
            return jnp.dot(h, win_ref[:, lo:lo + width], preferred_element_type=F32)

        u = seg(2 * c, c) * seg(0, c)
        p1, p2 = prev[7:8], prev[6:7]
        u1 = jnp.where(row == 0, p1, pltpu.roll(u, 1, 0))
        u2 = jnp.where(row == 0, p2, jnp.where(row == 1, p1, pltpu.roll(u, 2, 0)))
        prev = u[tg - 8:]
        conv = cw[2:3] * u + cw[1:2] * u1 + cw[0:1] * u2
        z = (seg(c, c) * conv).astype(BF16)
        yc = jnp.dot(z, wco_ref[...], preferred_element_type=F32)
        gc_ref[0, rows] = jax.nn.sigmoid(seg(3 * c + 3 * a, d)) * yc
        sga_ref[0, rows] = jax.nn.sigmoid(seg(3 * c + 3 * a + d, d))

        q = seg(3 * c, a)
        k = seg(3 * c + a, a)
        v = seg(3 * c + 2 * a, a)

        for bi in range(blocks_per_group):
            ks = jnp.sum(k[bi * MOBA_BLOCK:(bi + 1) * MOBA_BLOCK], axis=0, keepdims=True) * (1.0 / MOBA_BLOCK)
            km = jnp.where(rows_nb == blk0 + bi, ks, km)

        qt = q.T
        vt = v.T
        tok_blk = blk0 + lax.broadcasted_iota(jnp.int32, (MASK_SLOTS, tg), 1) // MOBA_BLOCK
        past = n_iota < tok_blk
        for hd in range(N_HEADS):
            qth = qt[hd * HEAD_DIM:(hd + 1) * HEAD_DIM]
            gate = jnp.dot(km[:, hd * HEAD_DIM:(hd + 1) * HEAD_DIM].astype(BF16), qth.astype(BF16),
                           preferred_element_type=F32)
            g = jnp.where(past, gate, NEG)
            sel = n_iota == tok_blk
            for _ in range(MOBA_TOPK):
                m = jnp.max(g, axis=0, keepdims=True)
                idx = jnp.min(jnp.where(g == m, n_iota, MASK_SLOTS), axis=0, keepdims=True)
                pick = n_iota == idx
                sel = sel | (pick & past)
                g = jnp.where(pick, -jnp.inf, g)
            qt_ref[0, hd, 0, :, rows] = jnp.concatenate(
                [qth * (HEAD_DIM ** -0.5 * LOG2E), jnp.where(sel, 0.0, NEG), zero_rows], axis=0).astype(BF16)
            vt_ref[0, hd, 0, :, rows] = jnp.concatenate(
                [vt[hd * HEAD_DIM:(hd + 1) * HEAD_DIM], ones_row], axis=0).astype(BF16)

        row_blk = blk0 + lax.broadcasted_iota(jnp.int32, (tg, LANES), 0) // MOBA_BLOCK
        k_pat = jnp.where(lane - HEAD_DIM == row_blk, 1.0, 0.0)
        for hp in range(N_HEADS // 2):
            kt = k[:, hp * LANES:(hp + 1) * LANES]
            k_ref[0, 2 * hp, rows] = jnp.where(low, kt, k_pat).astype(BF16)
            k_ref[0, 2 * hp + 1, rows] = jnp.where(low, pltpu.roll(kt, HEAD_DIM, 1), k_pat).astype(BF16)

    carry_ref[...] = prev
    kmean_ref[...] = km


def _projection(x, g, w_in, conv_w, w_conv_out):
    b, s, d = x.shape
    tm = TOKEN_TILE
    const = lambda shape: pl.BlockSpec(shape, lambda bi, si: (0,) * len(shape),
                                       pipeline_mode=pl.Buffered(1))
    tok = pl.BlockSpec((1, tm, d), lambda bi, si: (bi, si, 0))
    return pl.pallas_call(
        _proj_kernel,
        grid=(b, s // tm),
        in_specs=[tok, const((1, d)), const((d, N_IN)), const((CONV_K, CONV_WIDTH)),
                  const((CONV_WIDTH, d))],
        out_specs=[pl.BlockSpec((1, N_HEADS, 1, AUG, tm), lambda bi, si: (bi, 0, si, 0, 0)),
                   pl.BlockSpec((1, N_HEADS, tm, AUG), lambda bi, si: (bi, 0, si, 0)),
                   pl.BlockSpec((1, N_HEADS, 1, V_ROWS, tm), lambda bi, si: (bi, 0, si, 0, 0)),
                   tok, tok],
        out_shape=[jax.ShapeDtypeStruct((b, N_HEADS, s // tm, AUG, tm), BF16),
                   jax.ShapeDtypeStruct((b, N_HEADS, s, AUG), BF16),
                   jax.ShapeDtypeStruct((b, N_HEADS, s // tm, V_ROWS, tm), BF16),
                   jax.ShapeDtypeStruct((b, s, d), F32),
                   jax.ShapeDtypeStruct((b, s, d), F32)],
        scratch_shapes=[pltpu.VMEM((8, CONV_WIDTH), F32),
                        pltpu.VMEM((MASK_SLOTS, ATTN_WIDTH), F32)],
        compiler_params=pltpu.CompilerParams(
            dimension_semantics=("arbitrary", "arbitrary"), vmem_limit_bytes=VMEM_LIMIT),
        name="moba_projection",
    )(x, g, w_in, conv_w, w_conv_out)


def _attn_kernel(qa_ref, qb_ref, k_ref, vt_ref, brow_ref, oa_ref, ob_ref,
                 q_ref, s_ref, tmax_ref, m_ref, acc_ref, bias_ref, *, n_tiles):
    pr = pl.program_id(2)
    n_items = n_tiles + 1
    n_far = n_items - 2 * N_NEAR
    n_heads = qa_ref.shape[1]
    tile_a, tile_b = pr, n_tiles - 1 - pr

    zero_tile = n_heads * N_NEAR

    @pl.when((pl.program_id(1) == 0) & (pr == 0))
    def _():
        kb = lax.broadcasted_iota(jnp.int32, (SUPER, SUPER), 0) // MOBA_BLOCK
        qb = lax.broadcasted_iota(jnp.int32, (SUPER, SUPER), 1) // MOBA_BLOCK
        for hh in range(n_heads):
            for delta in range(N_NEAR):
                rows = jnp.broadcast_to(brow_ref[hh, delta], (SUPER, 2 * SUPER))
                tile = pltpu.roll(rows, 0, 1, stride=1, stride_axis=0)[:, :SUPER]
                dist = (SUPER // MOBA_BLOCK) * delta + qb - kb
                bias_ref[hh * N_NEAR + delta] = jnp.where(dist >= FAR_DIST, 0.0, tile)
        bias_ref[zero_tile] = jnp.zeros((SUPER, SUPER), F32)

    def item(w):
        far_b = jnp.minimum(tile_b + 1 - N_NEAR, n_far)
        rest_b = tile_b + 1 - far_b
        u = w - n_far
        is_far = w < n_far
        second = jnp.where(is_far, w < far_b, u < rest_b)
        t = jnp.where(second, jnp.where(is_far, w, far_b + u),
                      jnp.where(is_far, w - far_b, n_far - far_b + u - rest_b))
        return second.astype(jnp.int32), jnp.where(second, tile_b, tile_a), t

    def produce(w, slot, near, heads=None):
        second, qi, t = item(w)
        keys = pl.ds(pl.multiple_of(t * SUPER, SUPER), SUPER)
        for hh in (range(n_heads) if heads is None else heads):
            st = jnp.dot(k_ref[0, hh, keys, :], q_ref[second, hh], preferred_element_type=F32)
            if near:
                delta = qi - t
                st = st + bias_ref[jnp.where(delta < N_NEAR, hh * N_NEAR + delta, zero_tile)]
            s_ref[slot, hh] = st
            tmax_ref[slot, hh] = jnp.max(st, axis=0, keepdims=True)

    def consume(w, slot, heads=None):
        second, _, t = item(w)
        for hh in (range(n_heads) if heads is None else heads):
            m = m_ref[second, hh]
            m_new = jnp.maximum(m, tmax_ref[slot, hh])
            p = jnp.exp2(s_ref[slot, hh] - m_new).astype(BF16)
            pv = jnp.dot(vt_ref[0, hh, t], p, preferred_element_type=F32)
            acc_ref[second, hh] = jnp.exp2(m - m_new) * acc_ref[second, hh] + pv
            m_ref[second, hh] = m_new

    q_ref[0] = qa_ref[0, :, 0]
    q_ref[1] = qb_ref[0, :, 0]
    m_ref[...] = jnp.full(m_ref.shape, -jnp.inf, F32)
    acc_ref[...] = jnp.zeros(acc_ref.shape, F32)
    produce(0, 0, near=False)

    def pair(jj, _, near):
        for hh in range(n_heads):
            produce(2 * jj + 1, 1, near, [hh])
            consume(2 * jj, 0, [hh])
        for hh in range(n_heads):
            produce(2 * jj + 2, 0, near, [hh])
            consume(2 * jj + 1, 1, [hh])
        return _

    far_pairs = (n_far - 1) // 2
    lax.fori_loop(0, far_pairs, functools.partial(pair, near=False), 0, unroll=2)
    lax.fori_loop(far_pairs, n_items // 2, functools.partial(pair, near=True), 0, unroll=2)
    consume(n_items - 1, 0)

    for second, o_ref in enumerate((oa_ref, ob_ref)):
        outs = [(acc_ref[second, hh, :HEAD_DIM] / acc_ref[second, hh, HEAD_DIM:HEAD_DIM + 1]).T
                for hh in range(n_heads)]
        o_ref[0, 0, 0] = jnp.concatenate(outs, axis=-1).astype(o_ref.dtype)


def _attention(qt, k_aug, vt, bias_rows):
    b, _, n_tiles, _, _ = qt.shape
    s = n_tiles * SUPER
    hb = 4
    half = n_tiles // 2
    q_spec = lambda tile: pl.BlockSpec((1, hb, 1, AUG, SUPER), lambda hp, bi, pr: (bi, hp, tile(pr), 0, 0))
    o_spec = lambda tile: pl.BlockSpec((1, 1, 1, SUPER, hb * HEAD_DIM),
                                       lambda hp, bi, pr: (bi, tile(pr), hp, 0, 0))
    o_shape = jax.ShapeDtypeStruct((b, half, N_HEADS // hb, SUPER, hb * HEAD_DIM), BF16)
    return pl.pallas_call(
        functools.partial(_attn_kernel, n_tiles=n_tiles),
        grid=(N_HEADS // hb, b, half),
        in_specs=[q_spec(lambda pr: pr), q_spec(lambda pr: n_tiles - 1 - pr),
                  pl.BlockSpec((1, hb, s, AUG), lambda hp, bi, pr: (bi, hp, 0, 0)),
                  pl.BlockSpec((1, hb, n_tiles, V_ROWS, SUPER), lambda hp, bi, pr: (bi, hp, 0, 0, 0)),
                  pl.BlockSpec((hb, N_NEAR, 1, 2 * SUPER), lambda hp, bi, pr: (hp, 0, 0, 0))],
        out_specs=[o_spec(lambda pr: pr), o_spec(lambda pr: half - 1 - pr)],
        out_shape=[o_shape, o_shape],
        scratch_shapes=[pltpu.VMEM((2, hb, AUG, SUPER), BF16),
                        pltpu.VMEM((2, hb, SUPER, SUPER), F32),
                        pltpu.VMEM((2, hb, 1, SUPER), F32),
                        pltpu.VMEM((2, hb, 1, SUPER), F32),
                        pltpu.VMEM((2, hb, V_ROWS, SUPER), F32),
                        pltpu.VMEM((hb * N_NEAR + 1, SUPER, SUPER), F32)],
        compiler_params=pltpu.CompilerParams(
            dimension_semantics=("arbitrary", "arbitrary", "arbitrary"),
            vmem_limit_bytes=VMEM_LIMIT),
        name="moba_attention",
    )(qt, qt, k_aug, vt, bias_rows)


def _out_kernel(x_ref, att_lo_ref, att_hi_ref, gc_ref, sga_ref, wao_ref, wo_ref, gpost_ref, gpre_ref,
                w1_ref, w2_ref, gmpost_ref, o_ref):
    first_half = pl.program_id(1) < pl.num_programs(1) // 2
    att = jnp.where(first_half, att_lo_ref[0, 0], att_hi_ref[0, 0])
    att = jnp.concatenate([att[i] for i in range(att.shape[0])], axis=-1)
    tm = att.shape[0]
    rows = [slice(i * tm // ROW_GROUPS, (i + 1) * tm // ROW_GROUPS) for i in range(ROW_GROUPS)]
    ya = [jnp.dot(att[r], wao_ref[...], preferred_element_type=F32) for r in rows]
    m = [(gc_ref[0, r] + sga_ref[0, r] * y).astype(BF16) for r, y in zip(rows, ya)]
    mix = [jnp.dot(mi, wo_ref[...], preferred_element_type=F32) for mi in m]
    x1 = [x_ref[0, r] + _rms(mx, gpost_ref[...]) for r, mx in zip(rows, mix)]
    h2 = [_rms(xi, gpre_ref[...]).astype(BF16) for xi in x1]
    f = [jnp.zeros(xi.shape, F32) for xi in x1]
    for ci in range(D_FF // FF_CHUNK):
        cols = slice(ci * FF_CHUNK, (ci + 1) * FF_CHUNK)
        for i in range(ROW_GROUPS):
            act = jnp.dot(h2[i], w1_ref[:, cols], preferred_element_type=F32)
            act = jnp.square(jnp.maximum(act, 0.0)).astype(BF16)
            f[i] = f[i] + jnp.dot(act, w2_ref[cols, :], preferred_element_type=F32)
    for r, xi, fi in zip(rows, x1, f):
        o_ref[0, r] = xi + _rms(fi, gmpost_ref[...])


def _output(x, att_lo, att_hi, gc, sga, w_attn_out, w_o, g_post, g_mlp_pre, w1, w2, g_mlp_post):
    b, s, d = x.shape
    tm = TOKEN_TILE
    half = s // tm // 2
    const = lambda shape: pl.BlockSpec(shape, lambda bi, si: (0,) * len(shape),
                                       pipeline_mode=pl.Buffered(1))
    tok = pl.BlockSpec((1, tm, d), lambda bi, si: (bi, si, 0))
    att_spec = lambda tile: pl.BlockSpec((1, 1) + att_lo.shape[2:], lambda bi, si: (bi, tile(si), 0, 0, 0))
    return pl.pallas_call(
        _out_kernel,
        grid=(b, s // tm),
        in_specs=[tok,
                  att_spec(lambda si: jnp.minimum(si, half - 1)),
                  att_spec(lambda si: jnp.maximum(si - half, 0)),
                  tok, tok,
                  const((ATTN_WIDTH, d)), const((d, d)), const((1, d)), const((1, d)),
                  const((d, D_FF)), const((D_FF, d)), const((1, d))],
        out_specs=tok,
        out_shape=jax.ShapeDtypeStruct((b, s, d), F32),
        compiler_params=pltpu.CompilerParams(
            dimension_semantics=("arbitrary", "arbitrary"), vmem_limit_bytes=VMEM_LIMIT),
        name="moba_output_mlp",
    )(x, att_lo, att_hi, gc, sga, w_attn_out, w_o, g_post, g_mlp_pre, w1, w2, g_mlp_post)


def kernel(x, ln_mix_pre, w_in, conv_w, w_conv_out, w_attn_out, rel_bias, w_o, ln_mix_post,
           ln_mlp_pre, w_mlp_in, w_mlp_out, ln_mlp_post):
    depth = w_in.shape[0]
    b, s, d = x.shape
    assert d == D_MODEL and TOKEN_TILE == SUPER and s % (2 * SUPER) == 0 and s // SUPER >= 2 * N_NEAR
    assert MOBA_TOPK <= s // MOBA_BLOCK <= MASK_SLOTS
    bias_rows = _bias_rows(rel_bias)
    for l in range(depth):
        qt, k_aug, vt, gc, sga = _projection(
            x, ln_mix_pre[l][None], w_in[l].astype(BF16), conv_w[l], w_conv_out[l].astype(BF16))
        att_lo, att_hi = _attention(qt, k_aug, vt, bias_rows)
        x = _output(x, att_lo, att_hi, gc, sga, w_attn_out[l].astype(BF16), w_o[l].astype(BF16),
                    ln_mix_post[l][None], ln_mlp_pre[l][None], w_mlp_in[l].astype(BF16),
                    w_mlp_out[l].astype(BF16), ln_mlp_post[l][None])
    return x
```

```python
import functools
import math

import jax
import jax.numpy as jnp
from jax import lax
from jax.experimental import pallas as pl
from jax.experimental.pallas import tpu as pltpu

D_MODEL = 1024
CONV_WIDTH = D_MODEL // 2
CONV_K = 3
N_HEADS = 8
HEAD_DIM = 64
ATTN_WIDTH = N_HEADS * HEAD_DIM
MOBA_BLOCK = 256
MOBA_TOPK = 3
N_BUCKETS = 32
MAX_DISTANCE = 1024
D_FF = 4 * D_MODEL
RMS_EPS = 1e-6
N_IN = 3 * CONV_WIDTH + 3 * ATTN_WIDTH + 2 * D_MODEL
NEG = -1e30
LOG2E = math.log2(math.e)

LANES = 128
SUBLANES = 8
BF16_SUBLANES = 16
AUG = 2 * HEAD_DIM
MASK_SLOTS = (AUG - HEAD_DIM) // 2
V_ROWS = HEAD_DIM + BF16_SUBLANES
SUPER = 2 * MOBA_BLOCK
FAR_DIST = -(-(MAX_DISTANCE + MOBA_BLOCK) // MOBA_BLOCK)
N_NEAR = (FAR_DIST + 1) // 2
TOKEN_TILE = 512
FF_CHUNK = 1024
ROW_GROUPS = 2
PROJ_GROUPS = 2
ATTN_HEADS_PER_STEP = 4
VMEM_LIMIT = 56 * 1024 * 1024

F32 = jnp.float32
BF16 = jnp.bfloat16


def _rms(x, g):
    return (x * lax.rsqrt(jnp.mean(x * x, axis=-1, keepdims=True) + RMS_EPS)) * g


def _t5_causal_bucket(rel):
    n = jnp.maximum(rel, 0)
    max_exact = N_BUCKETS // 2
    nf = jnp.maximum(n, max_exact).astype(F32)
    large = max_exact + (jnp.log(nf / max_exact) / math.log(MAX_DISTANCE / max_exact)
                         * (N_BUCKETS - max_exact)).astype(jnp.int32)
    large = jnp.minimum(large, N_BUCKETS - 1)
    return jnp.where(n < max_exact, n, large)


def _bias_rows(rel_bias):
    m = jnp.arange(2 * SUPER)
    delta = jnp.arange(N_NEAR)[:, None]
    rel = delta * SUPER + jnp.where(m < SUPER, m, m - 2 * SUPER)
    far = rel_bias[:, _t5_causal_bucket(jnp.int32(FAR_DIST * MOBA_BLOCK))]
    w = jnp.where(rel >= 0, (rel_bias[:, _t5_causal_bucket(rel)] - far[:, None, None]) * LOG2E, NEG)
    return w.astype(F32)[:, :, None, :]


def _proj_kernel(x_ref, g_ref, win_ref, cw_ref, wco_ref,
                 qt_ref, k_ref, vt_ref, gc_ref, sga_ref,
                 carry_ref, kmean_ref):
    tm = x_ref.shape[1]
    tg = tm // PROJ_GROUPS
    c, a, d = CONV_WIDTH, ATTN_WIDTH, D_MODEL
    s = pl.program_id(1)
    assert tg % MOBA_BLOCK == 0
    blocks_per_group = tg // MOBA_BLOCK

    @pl.when(s == 0)
    def _():
        carry_ref[...] = jnp.zeros_like(carry_ref)
        kmean_ref[...] = jnp.zeros_like(kmean_ref)

    cw = cw_ref[...]
    rows_nb = lax.broadcasted_iota(jnp.int32, kmean_ref.shape, 0)
    row = lax.broadcasted_iota(jnp.int32, (tg, c), 0)
    n_iota = lax.broadcasted_iota(jnp.int32, (MASK_SLOTS, tg), 0)
    zero_rows = jnp.zeros((AUG - HEAD_DIM - MASK_SLOTS, tg), F32)
    ones_row = jnp.where(lax.broadcasted_iota(jnp.int32, (V_ROWS - HEAD_DIM, tg), 0) == 0, 1.0, 0.0)
    lane = lax.broadcasted_iota(jnp.int32, (tg, LANES), 1)
    low = lane < HEAD_DIM

    km = kmean_ref[...]
    prev = carry_ref[...]
    for gi in range(PROJ_GROUPS):
        rows = slice(gi * tg, (gi + 1) * tg)
        blk0 = (s * PROJ_GROUPS + gi) * blocks_per_group
        h = _rms(x_ref[0, rows], g_ref[...]).astype(BF16)

        def seg(lo, width, h=h):
            return jnp.dot(h, win_ref[:, lo:lo + width], preferred_element_type=F32)

        u = seg(2 * c, c) * seg(0, c)
        p1, p2 = prev[SUBLANES - 1:SUBLANES], prev[SUBLANES - 2:SUBLANES - 1]
        u1 = jnp.where(row == 0, p1, pltpu.roll(u, 1, 0))
        u2 = jnp.where(row == 0, p2, jnp.where(row == 1, p1, pltpu.roll(u, 2, 0)))
        prev = u[tg - SUBLANES:]
        conv = cw[2:3] * u + cw[1:2] * u1 + cw[0:1] * u2
        z = (seg(c, c) * conv).astype(BF16)
        yc = jnp.dot(z, wco_ref[...], preferred_element_type=F32)
        gc_ref[0, rows] = jax.nn.sigmoid(seg(3 * c + 3 * a, d)) * yc
        sga_ref[0, rows] = jax.nn.sigmoid(seg(3 * c + 3 * a + d, d))

        q = seg(3 * c, a)
        k = seg(3 * c + a, a)
        v = seg(3 * c + 2 * a, a)

        for bi in range(blocks_per_group):
            ks = jnp.sum(k[bi * MOBA_BLOCK:(bi + 1) * MOBA_BLOCK], axis=0, keepdims=True) * (1.0 / MOBA_BLOCK)
            km = jnp.where(rows_nb == blk0 + bi, ks, km)

        qt = q.T
        vt = v.T
        tok_blk = blk0 + lax.broadcasted_iota(jnp.int32, (MASK_SLOTS, tg), 1) // MOBA_BLOCK
        past = n_iota < tok_blk
        for hd in range(N_HEADS):
            qth = qt[hd * HEAD_DIM:(hd + 1) * HEAD_DIM]
            gate = jnp.dot(km[:, hd * HEAD_DIM:(hd + 1) * HEAD_DIM].astype(BF16), qth.astype(BF16),
                           preferred_element_type=F32)
            g = jnp.where(past, gate, NEG)
            sel = n_iota == tok_blk
            for _ in range(MOBA_TOPK):
                m = jnp.max(g, axis=0, keepdims=True)
                idx = jnp.min(jnp.where(g == m, n_iota, MASK_SLOTS), axis=0, keepdims=True)
                pick = n_iota == idx
                sel = sel | (pick & past)
                g = jnp.where(pick, -jnp.inf, g)
            qt_ref[0, hd, 0, :, rows] = jnp.concatenate(
                [qth * (HEAD_DIM ** -0.5 * LOG2E), jnp.where(sel, 0.0, NEG), zero_rows], axis=0).astype(BF16)
            vt_ref[0, hd, 0, :, rows] = jnp.concatenate(
                [vt[hd * HEAD_DIM:(hd + 1) * HEAD_DIM], ones_row], axis=0).astype(BF16)

        row_blk = blk0 + lax.broadcasted_iota(jnp.int32, (tg, LANES), 0) // MOBA_BLOCK
        k_pat = jnp.where(lane - HEAD_DIM == row_blk, 1.0, 0.0)
        for hp in range(N_HEADS // 2):
            kt = k[:, hp * LANES:(hp + 1) * LANES]
            k_ref[0, 2 * hp, rows] = jnp.where(low, kt, k_pat).astype(BF16)
            k_ref[0, 2 * hp + 1, rows] = jnp.where(low, pltpu.roll(kt, HEAD_DIM, 1), k_pat).astype(BF16)

    carry_ref[...] = prev
    kmean_ref[...] = km


def _projection(x, g, w_in, conv_w, w_conv_out):
    b, s, d = x.shape
    tm = TOKEN_TILE
    const = lambda shape: pl.BlockSpec(shape, lambda bi, si: (0,) * len(shape),
                                       pipeline_mode=pl.Buffered(1))
    tok = pl.BlockSpec((1, tm, d), lambda bi, si: (bi, si, 0))
    return pl.pallas_call(
        _proj_kernel,
        grid=(b, s // tm),
        in_specs=[tok, const((1, d)), const((d, N_IN)), const((CONV_K, CONV_WIDTH)),
                  const((CONV_WIDTH, d))],
        out_specs=[pl.BlockSpec((1, N_HEADS, 1, AUG, tm), lambda bi, si: (bi, 0, si, 0, 0)),
                   pl.BlockSpec((1, N_HEADS, tm, AUG), lambda bi, si: (bi, 0, si, 0)),
                   pl.BlockSpec((1, N_HEADS, 1, V_ROWS, tm), lambda bi, si: (bi, 0, si, 0, 0)),
                   tok, tok],
        out_shape=[jax.ShapeDtypeStruct((b, N_HEADS, s // tm, AUG, tm), BF16),
                   jax.ShapeDtypeStruct((b, N_HEADS, s, AUG), BF16),
                   jax.ShapeDtypeStruct((b, N_HEADS, s // tm, V_ROWS, tm), BF16),
                   jax.ShapeDtypeStruct((b, s, d), F32),
                   jax.ShapeDtypeStruct((b, s, d), F32)],
        scratch_shapes=[pltpu.VMEM((SUBLANES, CONV_WIDTH), F32),
                        pltpu.VMEM((MASK_SLOTS, ATTN_WIDTH), F32)],
        compiler_params=pltpu.CompilerParams(
            dimension_semantics=("arbitrary", "arbitrary"), vmem_limit_bytes=VMEM_LIMIT),
        name="moba_projection",
    )(x, g, w_in, conv_w, w_conv_out)


def _attn_kernel(qa_ref, qb_ref, k_ref, vt_ref, brow_ref, oa_ref, ob_ref,
                 q_ref, s_ref, tmax_ref, m_ref, acc_ref, bias_ref, *, n_tiles):
    pr = pl.program_id(2)
    n_items = n_tiles + 1
    n_far = n_items - 2 * N_NEAR
    n_heads = qa_ref.shape[1]
    tile_a, tile_b = pr, n_tiles - 1 - pr

    zero_tile = n_heads * N_NEAR

    @pl.when((pl.program_id(1) == 0) & (pr == 0))
    def _():
        kb = lax.broadcasted_iota(jnp.int32, (SUPER, SUPER), 0) // MOBA_BLOCK
        qb = lax.broadcasted_iota(jnp.int32, (SUPER, SUPER), 1) // MOBA_BLOCK
        for hh in range(n_heads):
            for delta in range(N_NEAR):
                rows = jnp.broadcast_to(brow_ref[hh, delta], (SUPER, 2 * SUPER))
                tile = pltpu.roll(rows, 0, 1, stride=1, stride_axis=0)[:, :SUPER]
                dist = (SUPER // MOBA_BLOCK) * delta + qb - kb
                bias_ref[hh * N_NEAR + delta] = jnp.where(dist >= FAR_DIST, 0.0, tile)
        bias_ref[zero_tile] = jnp.zeros((SUPER, SUPER), F32)

    def item(w):
        far_b = jnp.minimum(tile_b + 1 - N_NEAR, n_far)
        rest_b = tile_b + 1 - far_b
        u = w - n_far
        is_far = w < n_far
        second = jnp.where(is_far, w < far_b, u < rest_b)
        t = jnp.where(second, jnp.where(is_far, w, far_b + u),
                      jnp.where(is_far, w - far_b, n_far - far_b + u - rest_b))
        return second.astype(jnp.int32), jnp.where(second, tile_b, tile_a), t

    def produce(w, slot, near, heads=None):
        second, qi, t = item(w)
        keys = pl.ds(pl.multiple_of(t * SUPER, SUPER), SUPER)
        for hh in (range(n_heads) if heads is None else heads):
            st = jnp.dot(k_ref[0, hh, keys, :], q_ref[second, hh], preferred_element_type=F32)
            if near:
                delta = qi - t
                st = st + bias_ref[jnp.where(delta < N_NEAR, hh * N_NEAR + delta, zero_tile)]
            s_ref[slot, hh] = st
            tmax_ref[slot, hh] = jnp.max(st, axis=0, keepdims=True)

    def consume(w, slot, heads=None):
        second, _, t = item(w)
        for hh in (range(n_heads) if heads is None else heads):
            m = m_ref[second, hh]
            m_new = jnp.maximum(m, tmax_ref[slot, hh])
            p = jnp.exp2(s_ref[slot, hh] - m_new).astype(BF16)
            pv = jnp.dot(vt_ref[0, hh, t], p, preferred_element_type=F32)
            acc_ref[second, hh] = jnp.exp2(m - m_new) * acc_ref[second, hh] + pv
            m_ref[second, hh] = m_new

    q_ref[0] = qa_ref[0, :, 0]
    q_ref[1] = qb_ref[0, :, 0]
    m_ref[...] = jnp.full(m_ref.shape, -jnp.inf, F32)
    acc_ref[...] = jnp.zeros(acc_ref.shape, F32)
    produce(0, 0, near=False)

    def pair(jj, _, near):
        for hh in range(n_heads):
            produce(2 * jj + 1, 1, near, [hh])
            consume(2 * jj, 0, [hh])
        for hh in range(n_heads):
            produce(2 * jj + 2, 0, near, [hh])
            consume(2 * jj + 1, 1, [hh])
        return _

    far_pairs = (n_far - 1) // 2
    lax.fori_loop(0, far_pairs, functools.partial(pair, near=False), 0, unroll=2)
    lax.fori_loop(far_pairs, n_items // 2, functools.partial(pair, near=True), 0, unroll=2)
    consume(n_items - 1, 0)

    for second, o_ref in enumerate((oa_ref, ob_ref)):
        outs = [(acc_ref[second, hh, :HEAD_DIM] / acc_ref[second, hh, HEAD_DIM:HEAD_DIM + 1]).T
                for hh in range(n_heads)]
        o_ref[0, 0, 0] = jnp.concatenate(outs, axis=-1).astype(o_ref.dtype)


def _attention(qt, k_aug, vt, bias_rows):
    b, _, n_tiles, _, _ = qt.shape
    s = n_tiles * SUPER
    hb = ATTN_HEADS_PER_STEP
    half = n_tiles // 2
    q_spec = lambda tile: pl.BlockSpec((1, hb, 1, AUG, SUPER), lambda hp, bi, pr: (bi, hp, tile(pr), 0, 0))
    o_spec = lambda tile: pl.BlockSpec((1, 1, 1, SUPER, hb * HEAD_DIM),
                                       lambda hp, bi, pr: (bi, tile(pr), hp, 0, 0))
    o_shape = jax.ShapeDtypeStruct((b, half, N_HEADS // hb, SUPER, hb * HEAD_DIM), BF16)
    return pl.pallas_call(
        functools.partial(_attn_kernel, n_tiles=n_tiles),
        grid=(N_HEADS // hb, b, half),
        in_specs=[q_spec(lambda pr: pr), q_spec(lambda pr: n_tiles - 1 - pr),
                  pl.BlockSpec((1, hb, s, AUG), lambda hp, bi, pr: (bi, hp, 0, 0)),
                  pl.BlockSpec((1, hb, n_tiles, V_ROWS, SUPER), lambda hp, bi, pr: (bi, hp, 0, 0, 0)),
                  pl.BlockSpec((hb, N_NEAR, 1, 2 * SUPER), lambda hp, bi, pr: (hp, 0, 0, 0))],
        out_specs=[o_spec(lambda pr: pr), o_spec(lambda pr: half - 1 - pr)],
        out_shape=[o_shape, o_shape],
        scratch_shapes=[pltpu.VMEM((2, hb, AUG, SUPER), BF16),
                        pltpu.VMEM((2, hb, SUPER, SUPER), F32),
                        pltpu.VMEM((2, hb, 1, SUPER), F32),
                        pltpu.VMEM((2, hb, 1, SUPER), F32),
                        pltpu.VMEM((2, hb, V_ROWS, SUPER), F32),
                        pltpu.VMEM((hb * N_NEAR + 1, SUPER, SUPER), F32)],
        compiler_params=pltpu.CompilerParams(
            dimension_semantics=("arbitrary", "arbitrary", "arbitrary"),
            vmem_limit_bytes=VMEM_LIMIT),
        name="moba_attention",
    )(qt, qt, k_aug, vt, bias_rows)


def _out_kernel(x_ref, att_lo_ref, att_hi_ref, gc_ref, sga_ref, wao_ref, wo_ref, gpost_ref, gpre_ref,
                w1_ref, w2_ref, gmpost_ref, o_ref):
    first_half = pl.program_id(1) < pl.num_programs(1) // 2
    att = jnp.where(first_half, att_lo_ref[0, 0], att_hi_ref[0, 0])
    att = jnp.concatenate([att[i] for i in range(att.shape[0])], axis=-1)
    tm = att.shape[0]
    rows = [slice(i * tm // ROW_GROUPS, (i + 1) * tm // ROW_GROUPS) for i in range(ROW_GROUPS)]
    ya = [jnp.dot(att[r], wao_ref[...], preferred_element_type=F32) for r in rows]
    m = [(gc_ref[0, r] + sga_ref[0, r] * y).astype(BF16) for r, y in zip(rows, ya)]
    mix = [jnp.dot(mi, wo_ref[...], preferred_element_type=F32) for mi in m]
    x1 = [x_ref[0, r] + _rms(mx, gpost_ref[...]) for r, mx in zip(rows, mix)]
    h2 = [_rms(xi, gpre_ref[...]).astype(BF16) for xi in x1]
    f = [jnp.zeros(xi.shape, F32) for xi in x1]
    for ci in range(D_FF // FF_CHUNK):
        cols = slice(ci * FF_CHUNK, (ci + 1) * FF_CHUNK)
        for i in range(ROW_GROUPS):
            act = jnp.dot(h2[i], w1_ref[:, cols], preferred_element_type=F32)
            act = jnp.square(jnp.maximum(act, 0.0)).astype(BF16)
            f[i] = f[i] + jnp.dot(act, w2_ref[cols, :], preferred_element_type=F32)
    for r, xi, fi in zip(rows, x1, f):
        o_ref[0, r] = xi + _rms(fi, gmpost_ref[...])


def _output(x, att_lo, att_hi, gc, sga, w_attn_out, w_o, g_post, g_mlp_pre, w1, w2, g_mlp_post):
    b, s, d = x.shape
    tm = TOKEN_TILE
    half = s // tm // 2
    const = lambda shape: pl.BlockSpec(shape, lambda bi, si: (0,) * len(shape),
                                       pipeline_mode=pl.Buffered(1))
    tok = pl.BlockSpec((1, tm, d), lambda bi, si: (bi, si, 0))
    att_spec = lambda tile: pl.BlockSpec((1, 1) + att_lo.shape[2:], lambda bi, si: (bi, tile(si), 0, 0, 0))
    return pl.pallas_call(
        _out_kernel,
        grid=(b, s // tm),
        in_specs=[tok,
                  att_spec(lambda si: jnp.minimum(si, half - 1)),
                  att_spec(lambda si: jnp.maximum(si - half, 0)),
                  tok, tok,
                  const((ATTN_WIDTH, d)), const((d, d)), const((1, d)), const((1, d)),
                  const((d, D_FF)), const((D_FF, d)), const((1, d))],
        out_specs=tok,
        out_shape=jax.ShapeDtypeStruct((b, s, d), F32),
        compiler_params=pltpu.CompilerParams(
            dimension_semantics=("arbitrary", "arbitrary"), vmem_limit_bytes=VMEM_LIMIT),
        name="moba_output_mlp",
    )(x, att_lo, att_hi, gc, sga, w_attn_out, w_o, g_post, g_mlp_pre, w1, w2, g_mlp_post)


def kernel(x, ln_mix_pre, w_in, conv_w, w_conv_out, w_attn_out, rel_bias, w_o, ln_mix_post,
           ln_mlp_pre, w_mlp_in, w_mlp_out, ln_mlp_post):
    depth = w_in.shape[0]
    b, s, d = x.shape
    assert d == D_MODEL and TOKEN_TILE == SUPER and s % (2 * SUPER) == 0 and s // SUPER >= 2 * N_NEAR
    assert MOBA_TOPK <= s // MOBA_BLOCK <= MASK_SLOTS
    bias_rows = _bias_rows(rel_bias)
    for l in range(depth):
        qt, k_aug, vt, gc, sga = _projection(
            x, ln_mix_pre[l][None], w_in[l].astype(BF16), conv_w[l], w_conv_out[l].astype(BF16))
        att_lo, att_hi = _attention(qt, k_aug, vt, bias_rows)
        x = _output(x, att_lo, att_hi, gc, sga, w_attn_out[l].astype(BF16), w_o[l].astype(BF16),
                    ln_mix_post[l][None], ln_mlp_pre[l][None], w_mlp_in[l].astype(BF16),
                    w_mlp_out[l].astype(BF16), ln_mlp_post[l][None])
    return x
```
